```python
import math
import jax, jax.numpy as jnp
from jax import lax
import numpy as np

D_MODEL = 1024
BATCH = 8
SEQ = 2048
DEPTH = 2

BRANCH_WIDTH = D_MODEL // 2
GLA_HEADS = 4
GLA_DV = BRANCH_WIDTH // GLA_HEADS
GLA_DK = GLA_DV // 2
GLA_LOW_RANK = 16
GLA_GATE_NORMALIZER = 16.0
GLA_CHUNK = 16
POOL_WINDOWS = (2, 4, 8, 16)
POOL_GROUPS = 4
POOL_GROUP_DIM = BRANCH_WIDTH // POOL_GROUPS
MOBA_HEADS = 4
MOBA_HEAD_DIM = BRANCH_WIDTH // MOBA_HEADS
MOBA_BLOCK = 256
MOBA_TOPK = 3
MOBA_Q_CHUNK = 16
REL_BUCKETS = 32
REL_MAX_EXACT = REL_BUCKETS // 2
REL_MAX_DIST = 128
D_FF = ((8 * D_MODEL + 767) // 768) * 256
NORM_EPS = 1e-6
NEG_INF = -1e30
IN_SPLITS = (GLA_HEADS * GLA_DK, GLA_HEADS * GLA_DK, GLA_HEADS * GLA_DV, GLA_LOW_RANK, GLA_HEADS * GLA_DV,
             BRANCH_WIDTH, MOBA_HEADS * MOBA_HEAD_DIM, MOBA_HEADS * MOBA_HEAD_DIM, MOBA_HEADS * MOBA_HEAD_DIM,
             D_MODEL, D_MODEL, D_MODEL)
IN_COLS = sum(IN_SPLITS)

kernel_name = "hybrid_gla_pool_moba_gated_block"


def rms_norm(x, w):
    x32 = x.astype(jnp.float32)
    y = x32 * lax.rsqrt(jnp.mean(x32 * x32, axis=-1, keepdims=True) + NORM_EPS)
    return (y * w.astype(jnp.float32)).astype(x.dtype)


def split_columns(proj):
    parts, off = [], 0
    for size in IN_SPLITS:
        parts.append(proj[..., off:off + size])
        off += size
    return parts


def t5_bucket(dist):
    n = jnp.maximum(dist, 0)
    large = REL_MAX_EXACT + (jnp.log(jnp.maximum(n, 1).astype(jnp.float32) / REL_MAX_EXACT)
                             / math.log(REL_MAX_DIST / REL_MAX_EXACT)
                             * (REL_BUCKETS - REL_MAX_EXACT)).astype(jnp.int32)
    large = jnp.minimum(large, REL_BUCKETS - 1)
    return jnp.where(n < REL_MAX_EXACT, n, large)


def gla_mix(q, k, v, g_lr, r, wg2, bg, norm_w):
    B_, S_ = q.shape[0], q.shape[1]
    f32 = jnp.float32
    glog = jax.nn.log_sigmoid((g_lr @ wg2).astype(f32) + bg.astype(f32)) / GLA_GATE_NORMALIZER
    N = S_ // GLA_CHUNK

    def chunks(t, d):
        return t.astype(f32).reshape(B_, N, GLA_CHUNK, GLA_HEADS, d).transpose(0, 3, 1, 2, 4)

    qc = chunks(q, GLA_DK) * (GLA_DK ** -0.5)
    kc = chunks(k, GLA_DK)
    vc = chunks(v, GLA_DV)
    b = jnp.cumsum(chunks(glog, GLA_DK), axis=3)
    causal = jnp.tril(jnp.ones((GLA_CHUNK, GLA_CHUNK), dtype=bool))
    diff = b[:, :, :, :, None, :] - b[:, :, :, None, :, :]
    decay = jnp.exp(jnp.where(causal[:, :, None], diff, -jnp.inf))
    attn = jnp.einsum('bhnid,bhnjd,bhnijd->bhnij', qc, kc, decay)
    o_intra = jnp.einsum('bhnij,bhnje->bhnie', attn, vc)
    b_last = b[:, :, :, -1]
    k_tail = kc * jnp.exp(b_last[:, :, :, None, :] - b)
    chunk_state = jnp.einsum('bhncd,bhnce->nbhde', k_tail, vc)
    chunk_decay = jnp.exp(b_last).transpose(2, 0, 1, 3)

    def step(state, inp):
        dec, cs = inp
        return dec[..., None] * state + cs, state

    init = jnp.zeros((B_, GLA_HEADS, GLA_DK, GLA_DV), f32)
    _, s_prev = lax.scan(step, init, (chunk_decay, chunk_state))
    o_inter = jnp.einsum('bhncd,nbhde->bhnce', qc * jnp.exp(b), s_prev)
    o = (o_intra + o_inter).reshape(B_, GLA_HEADS, S_, GLA_DV)
    o = rms_norm(o, norm_w)
    o = o.transpose(0, 2, 1, 3).reshape(B_, S_, GLA_HEADS * GLA_DV)
    return (o * jax.nn.silu(r.astype(f32))).astype(q.dtype)


def pool_mix(u, pool_w, pool_scale):
    B_, S_ = u.shape[0], u.shape[1]
    ug = u.astype(jnp.float32).reshape(B_, S_, POOL_GROUPS, POOL_GROUP_DIM)
    cs = jnp.cumsum(ug, axis=1)
    t = jnp.arange(S_)
    outs = []
    for gi, w in enumerate(POOL_WINDOWS):
        cg = cs[:, :, gi]
        prev = jnp.pad(cg, ((0, 0), (w, 0), (0, 0)))[:, :S_]
        cnt = jnp.minimum(t + 1, w).astype(jnp.float32)[None, :, None]
        outs.append((cg - prev) / cnt - ug[:, :, gi])
    p = jnp.stack(outs, axis=2)
    y = jnp.einsum('bsgc,gce->bsge', p, pool_w.astype(jnp.float32)).reshape(B_, S_, BRANCH_WIDTH)
    return (y * pool_scale.astype(jnp.float32)).astype(u.dtype)


def moba_mix(q, k, v, qn_w, kn_w, rel_bias):
    B_, S_ = q.shape[0], q.shape[1]
    f32 = jnp.float32

    def heads(t):
        return t.astype(f32).reshape(B_, S_, MOBA_HEADS, MOBA_HEAD_DIM).transpose(0, 2, 1, 3)

    qh = rms_norm(heads(q), qn_w)
    kh = rms_norm(heads(k), kn_w)
    vh = heads(v)
    nb = -(-S_ // MOBA_BLOCK)
    pad = nb * MOBA_BLOCK - S_
    kb = jnp.pad(kh, ((0, 0), (0, 0), (0, pad), (0, 0))).reshape(B_, MOBA_HEADS, nb, MOBA_BLOCK, MOBA_HEAD_DIM)
    vb = jnp.pad(vh, ((0, 0), (0, 0), (0, pad), (0, 0))).reshape(B_, MOBA_HEADS, nb, MOBA_BLOCK, MOBA_HEAD_DIM)
    kmean = jnp.mean(kb, axis=3)
    topk = min(MOBA_TOPK, nb)
    scale = MOBA_HEAD_DIM ** -0.5
    bias_hb = rel_bias.astype(f32).T
    bidx = jnp.arange(B_)[:, None, None, None]
    hidx = jnp.arange(MOBA_HEADS)[None, :, None, None]
    n_qc = S_ // MOBA_Q_CHUNK
    qc = qh.reshape(B_, MOBA_HEADS, n_qc, MOBA_Q_CHUNK, MOBA_HEAD_DIM).transpose(2, 0, 1, 3, 4)
    blk_ar = jnp.arange(MOBA_BLOCK)

    def chunk(args):
        ci, qq = args
        start = ci * MOBA_Q_CHUNK
        pos = start + jnp.arange(MOBA_Q_CHUNK)
        qblk = start // MOBA_BLOCK
        scores = jnp.einsum('bhqd,bhnd->bhqn', qq, kmean)
        scores = jnp.where(jnp.arange(nb) < qblk, scores, -jnp.inf)
        _, sel = lax.top_k(scores, topk)
        valid = sel < qblk
        kg = kb[bidx, hidx, sel]
        vg = vb[bidx, hidx, sel]
        kpos = sel[..., None] * MOBA_BLOCK + blk_ar
        bias_sel = bias_hb[hidx[..., None], t5_bucket(pos[None, None, :, None, None] - kpos)]
        lg_sel = jnp.einsum('bhqd,bhqkjd->bhqkj', qq, kg) * scale + bias_sel
        lg_sel = jnp.where(valid[..., None], lg_sel, NEG_INF)
        k_own = lax.dynamic_index_in_dim(kb, qblk, axis=2, keepdims=False)
        v_own = lax.dynamic_index_in_dim(vb, qblk, axis=2, keepdims=False)
        dist = pos[:, None] - (qblk * MOBA_BLOCK + blk_ar)[None, :]
        bias_own = bias_hb[:, t5_bucket(dist)]
        lg_own = jnp.einsum('bhqd,bhjd->bhqj', qq, k_own) * scale + bias_own
        lg_own = jnp.where(dist >= 0, lg_own, NEG_INF)
        lg = jnp.concatenate([lg_sel.reshape(B_, MOBA_HEADS, MOBA_Q_CHUNK, topk * MOBA_BLOCK), lg_own], axis=-1)
        p = jax.nn.softmax(lg, axis=-1)
        p_sel = p[..., :topk * MOBA_BLOCK].reshape(B_, MOBA_HEADS, MOBA_Q_CHUNK, topk, MOBA_BLOCK)
        p_own = p[..., topk * MOBA_BLOCK:]
        return (jnp.einsum('bhqkj,bhqkjd->bhqd', p_sel, vg)
                + jnp.einsum('bhqj,bhjd->bhqd', p_own, v_own))

    out = lax.map(chunk, (jnp.arange(n_qc, dtype=jnp.int32), qc))
    out = out.transpose(1, 0, 3, 2, 4).reshape(B_, S_, MOBA_HEADS * MOBA_HEAD_DIM)
    return out.astype(q.dtype)


def setup_inputs(seed: int = 0) -> dict:
    key = jax.random.key(seed)
    ks = jax.random.split(key, 20)
    nrm = jax.random.normal
    res_scale = (2 * DEPTH) ** -0.5
    return {
        "x": nrm(ks[0], (BATCH, SEQ, D_MODEL), jnp.float32),
        "norm1_w": 1.0 + 0.02 * nrm(ks[1], (DEPTH, D_MODEL), jnp.float32),
        "w_in": nrm(ks[2], (DEPTH, D_MODEL, IN_COLS), jnp.float32) * D_MODEL ** -0.5,
        "gla_wg2": nrm(ks[3], (DEPTH, GLA_LOW_RANK, GLA_HEADS * GLA_DK), jnp.float32) * GLA_LOW_RANK ** -0.5,
        "gla_bg": 0.1 * nrm(ks[4], (DEPTH, GLA_HEADS * GLA_DK), jnp.float32),
        "gla_norm_w": 1.0 + 0.02 * nrm(ks[5], (DEPTH, GLA_DV), jnp.float32),
        "pool_w": nrm(ks[6], (DEPTH, POOL_GROUPS, POOL_GROUP_DIM, POOL_GROUP_DIM), jnp.float32) * POOL_GROUP_DIM ** -0.5,
        "pool_scale": 1.0 + 0.02 * nrm(ks[7], (DEPTH, BRANCH_WIDTH), jnp.float32),
        "moba_qn_w": 1.0 + 0.02 * nrm(ks[8], (DEPTH, MOBA_HEAD_DIM), jnp.float32),
        "moba_kn_w": 1.0 + 0.02 * nrm(ks[9], (DEPTH, MOBA_HEAD_DIM), jnp.float32),
        "rel_bias": 0.5 * nrm(ks[10], (REL_BUCKETS, MOBA_HEADS), jnp.float32),
        "w_up_a": nrm(ks[11], (DEPTH, GLA_HEADS * GLA_DV, D_MODEL), jnp.float32) * (GLA_HEADS * GLA_DV) ** -0.5,
        "w_up_b": nrm(ks[12], (DEPTH, BRANCH_WIDTH, D_MODEL), jnp.float32) * BRANCH_WIDTH ** -0.5,
        "w_up_c": nrm(ks[13], (DEPTH, MOBA_HEADS * MOBA_HEAD_DIM, D_MODEL), jnp.float32) * (MOBA_HEADS * MOBA_HEAD_DIM) ** -0.5,
        "w_out": nrm(ks[14], (DEPTH, D_MODEL, D_MODEL), jnp.float32) * D_MODEL ** -0.5 * res_scale,
        "norm2_w": 1.0 + 0.02 * nrm(ks[15], (DEPTH, D_MODEL), jnp.float32),
        "ffn_w_gate": nrm(ks[16], (DEPTH, D_MODEL, D_FF), jnp.float32) * D_MODEL ** -0.5,
        "ffn_w_up": nrm(ks[17], (DEPTH, D_MODEL, D_FF), jnp.float32) * D_MODEL ** -0.5,
        "ffn_w_down": nrm(ks[18], (DEPTH, D_FF, D_MODEL), jnp.float32) * D_FF ** -0.5 * res_scale,
    }


def reference(x, norm1_w, w_in, gla_wg2, gla_bg, gla_norm_w, pool_w, pool_scale, moba_qn_w, moba_kn_w,
              rel_bias, w_up_a, w_up_b, w_up_c, w_out, norm2_w, ffn_w_gate, ffn_w_up, ffn_w_down):
    for l in range(DEPTH):
        h = rms_norm(x, norm1_w[l])
        (gq, gk, gv, g_lr, gr, pu, mq, mk, mv, gate_a, gate_b, gate_c) = split_columns(h @ w_in[l])
        ya = gla_mix(gq, gk, gv, g_lr, gr, gla_wg2[l], gla_bg[l], gla_norm_w[l])
        yb = pool_mix(pu, pool_w[l], pool_scale[l])
        yc = moba_mix(mq, mk, mv, moba_qn_w[l], moba_kn_w[l], rel_bias)
        merged = (jax.nn.sigmoid(gate_a) * (ya @ w_up_a[l])
                  + jax.nn.sigmoid(gate_b) * (yb @ w_up_b[l])
                  + jax.nn.sigmoid(gate_c) * (yc @ w_up_c[l]))
        x = x + merged @ w_out[l]
        h2 = rms_norm(x, norm2_w[l])
        x = x + (jax.nn.silu(h2 @ ffn_w_gate[l]) * (h2 @ ffn_w_up[l])) @ ffn_w_down[l]
    return x
```

```python
import functools
import math

import numpy as np
import jax
import jax.numpy as jnp
from jax import lax
from jax.experimental import pallas as pl
from jax.experimental.pallas import tpu as pltpu

D_MODEL = 1024
DEPTH = 2
BRANCH_WIDTH = D_MODEL // 2
GLA_HEADS = 4
GLA_DV = BRANCH_WIDTH // GLA_HEADS
GLA_DK = GLA_DV // 2
GLA_LOW_RANK = 16
GLA_GATE_NORMALIZER = 16.0
POOL_WINDOWS = (2, 4, 8, 16)
POOL_GROUPS = 4
POOL_GROUP_DIM = BRANCH_WIDTH // POOL_GROUPS
MOBA_HEADS = 4
MOBA_HEAD_DIM = BRANCH_WIDTH // MOBA_HEADS
MOBA_BLOCK = 256
MOBA_TOPK = 3
REL_BUCKETS = 32
REL_MAX_EXACT = REL_BUCKETS // 2
REL_MAX_DIST = 128
D_FF = ((8 * D_MODEL + 767) // 768) * 256
NORM_EPS = 1e-6
NEG_INF = -1e30

LANES = 128
MXU_DIM = 256
VMEM_LIMIT_BYTES = 56 * 1024 * 1024

_SRC = dict(gq=(0, 256), gk=(256, 512), gv=(512, 1024), glr=(1024, 1040), gr=(1040, 1552), pu=(1552, 2064),
            mq=(2064, 2576), mk=(2576, 3088), mv=(3088, 3600), ga=(3600, 4624), gb=(4624, 5648), gc=(5648, 6672))
_ORDER = ("ga", "gb", "gc", "gv", "gr", "pu", "mq", "mk", "mv", "gq", "gk", "glr")
_WIDTH = dict(ga=1024, gb=1024, gc=1024, gv=512, gr=512, pu=512, mq=512, mk=512, mv=512, gq=256, gk=256, glr=LANES)
_OFF = {}
_o = 0
for _n in _ORDER:
    assert _o % _WIDTH[_n] == 0
    _OFF[_n] = _o
    _o += _WIDTH[_n]
PROJ_TN = 768
PROJ_COLS = -(-_o // PROJ_TN) * PROJ_TN

GLA_T = 128
GLA_LEVELS = 7
FFN_CHUNK = 256
FFN_NCHUNK = D_FF // FFN_CHUNK

F32 = jnp.float32
BF16 = jnp.bfloat16


def _params(*sem):
    return pltpu.CompilerParams(dimension_semantics=sem, vmem_limit_bytes=VMEM_LIMIT_BYTES)


def _dot(a, b):
    return jnp.dot(a, b, preferred_element_type=F32)


def _dot_nt(a, b):
    return lax.dot_general(a, b, (((1,), (1,)), ((), ())), preferred_element_type=F32)


def _split_hi_lo(x):
    hi = x.astype(BF16)
    lo = (x - hi.astype(F32)).astype(BF16)
    return hi, lo


def _rms(x, w):
    return x * lax.rsqrt(jnp.mean(x * x, axis=-1, keepdims=True) + NORM_EPS) * w


def _inproj_kernel(x_ref, nw_ref, w_ref, o_ref, h_ref):
    @pl.when(pl.program_id(1) == 0)
    def _():
        h_ref[...] = _rms(x_ref[...], nw_ref[...]).astype(BF16)

    o_ref[...] = _dot(h_ref[...], w_ref[...])


def _inproj(x2d, norm_w, w_p, tm):
    m = x2d.shape[0]
    return pl.pallas_call(
        _inproj_kernel,
        out_shape=jax.ShapeDtypeStruct((m, PROJ_COLS), F32),
        grid=(m // tm, PROJ_COLS // PROJ_TN),
        in_specs=[pl.BlockSpec((tm, D_MODEL), lambda i, j: (i, 0)),
                  pl.BlockSpec((1, D_MODEL), lambda i, j: (0, 0)),
                  pl.BlockSpec((D_MODEL, PROJ_TN), lambda i, j: (0, j))],
        out_specs=pl.BlockSpec((tm, PROJ_TN), lambda i, j: (i, j)),
        scratch_shapes=[pltpu.VMEM((tm, D_MODEL), BF16)],
        compiler_params=_params("parallel", "arbitrary"),
        name="inproj",
    )(x2d, norm_w.reshape(1, D_MODEL), w_p)


def _gla_constants():
    t = np.arange(GLA_T)
    u = np.arange(GLA_T)
    a_rows = []
    masks = []
    for lev in range(GLA_LEVELS):
        s = (GLA_T // 2) >> lev
        g = 2 * s
        first = (t % g) < s
        ref = (t // g) * g + s - 1
        a = np.where(first[:, None],
                     (u[None, :] > t[:, None]) & (u[None, :] <= ref[:, None]),
                     (u[None, :] > ref[:, None]) & (u[None, :] <= t[:, None]))
        a_rows.append(a)
        same = (t[:, None] // g) == (t[None, :] // g)
        masks.append(same & (~first)[:, None] & first[None, :])
    a_rows.append(u[None, :] <= t[:, None])
    masks.append(np.eye(GLA_T, dtype=bool))
    a_all = np.concatenate(a_rows, axis=0).astype(np.float32)
    akt = np.concatenate([(u[:, None] > t[None, :]), np.ones((GLA_T, GLA_T), bool)], axis=1).astype(np.float32)
    return a_all, akt, np.stack(masks).astype(np.float32)


def _log_sigmoid(z):
    return jnp.minimum(z, 0.0) - jnp.log1p(jnp.exp(-jnp.abs(z)))


def _gla_kernel(q_ref, k_ref, v_ref, r_ref, glr_ref, wg2_ref, wg2t_ref, bgr_ref, bgc_ref, a_ref, akt_ref,
                mask_ref, nw_ref, o_ref, s_ref, sbd_ref):
    nh, dk, dv, t = GLA_HEADS, GLA_DK, GLA_DV, GLA_T

    @pl.when(pl.program_id(1) == 0)
    def _():
        s_ref[...] = jnp.zeros_like(s_ref)
        sbd_ref[...] = jnp.zeros_like(sbd_ref)

    q8 = q_ref[0] * (dk ** -0.5)
    k = k_ref[0]
    v_bf = v_ref[0].astype(BF16)
    glr_bf = glr_ref[0].astype(BF16)

    glog = _log_sigmoid(_dot(glr_bf, wg2_ref[...]) + bgr_ref[...]) * (1.0 / GLA_GATE_NORMALIZER)
    glog_t = _log_sigmoid(_dot_nt(wg2t_ref[...], glr_bf) + bgc_ref[...]) * (1.0 / GLA_GATE_NORMALIZER)

    g_hi, g_lo = _split_hi_lo(glog)
    a_all = a_ref[...]
    e_all = _dot(a_all, g_hi) + _dot(a_all, g_lo)
    gt_hi, gt_lo = _split_hi_lo(glog_t)
    akt = akt_ref[...]
    et_all = _dot(gt_hi, akt) + _dot(gt_lo, akt)

    qe = (q8 * jnp.exp(e_all[GLA_LEVELS * t:(GLA_LEVELS + 1) * t])).astype(BF16)
    o_inter = _dot(qe, sbd_ref[...])

    lane_head = lax.broadcasted_iota(jnp.int32, (t, nh * dk), 1) // dk
    head_masks = [lane_head == h for h in range(nh)]
    attn = [jnp.zeros((t, t), F32) for _ in range(nh)]
    for lev in range(GLA_LEVELS + 1):
        if lev < GLA_LEVELS:
            e = jnp.exp(e_all[lev * t:(lev + 1) * t])
            ql = (q8 * e).astype(BF16)
            kl = (k * e).astype(BF16)
        else:
            ql = q8.astype(BF16)
            kl = k.astype(BF16)
        kbd = jnp.concatenate([jnp.where(head_masks[h], kl, jnp.zeros_like(kl)) for h in range(nh)], axis=0)
        sc = _dot_nt(ql, kbd)
        m = mask_ref[lev]
        for h in range(nh):
            attn[h] = attn[h] + sc[:, h * t:(h + 1) * t] * m

    nw = nw_ref[...]
    r = r_ref[0]
    outs = []
    for h in range(nh):
        oh = _dot(attn[h].astype(BF16), v_bf[:, h * dv:(h + 1) * dv]) + o_inter[:, h * dv:(h + 1) * dv]
        outs.append(_rms(oh, nw))
    o = jnp.concatenate(outs, axis=1)
    o_ref[0] = o * (r * (1.0 / (1.0 + jnp.exp(-r))))

    k_tail_t = (k.T * jnp.exp(et_all[:, :t])).astype(BF16)
    dec_t = jnp.exp(et_all[:, t:])
    for h in range(nh):
        upd = _dot(k_tail_t[h * dk:(h + 1) * dk, :], v_bf[:, h * dv:(h + 1) * dv])
        s_new = dec_t[h * dk:(h + 1) * dk, :] * s_ref[h] + upd
        s_ref[h] = s_new
        sbd_ref[h * dk:(h + 1) * dk, h * dv:(h + 1) * dv] = s_new.astype(BF16)


def _gla(proj, wg2, bg, norm_w):
    b, s, _ = proj.shape
    t = GLA_T
    hk = GLA_HEADS * GLA_DK
    hv = GLA_HEADS * GLA_DV
    a_all, akt, masks = _gla_constants()
    wg2_p = jnp.zeros((LANES, hk), F32).at[:GLA_LOW_RANK].set(wg2).astype(BF16)
    wg2_t = wg2_p.T
    const = lambda *shape: pl.BlockSpec(shape, lambda i, j: (0,) * len(shape))
    return pl.pallas_call(
        _gla_kernel,
        out_shape=jax.ShapeDtypeStruct((b, s, hv), F32),
        grid=(b, s // t),
        in_specs=[pl.BlockSpec((1, t, hk), lambda i, j: (i, j, _OFF["gq"] // hk)),
                  pl.BlockSpec((1, t, hk), lambda i, j: (i, j, _OFF["gk"] // hk)),
                  pl.BlockSpec((1, t, hv), lambda i, j: (i, j, _OFF["gv"] // hv)),
                  pl.BlockSpec((1, t, hv), lambda i, j: (i, j, _OFF["gr"] // hv)),
                  pl.BlockSpec((1, t, LANES), lambda i, j: (i, j, _OFF["glr"] // LANES)),
                  const(LANES, hk), const(hk, LANES), const(1, hk), const(hk, 1),
                  const((GLA_LEVELS + 1) * t, t), const(t, 2 * t), const(GLA_LEVELS + 1, t, t),
                  const(1, GLA_DV)],
        out_specs=pl.BlockSpec((1, t, hv), lambda i, j: (i, j, 0)),
        scratch_shapes=[pltpu.VMEM((GLA_HEADS, GLA_DK, GLA_DV), F32), pltpu.VMEM((hk, hv), BF16)],
        compiler_params=_params("parallel", "arbitrary"),
        name="gla",
    )(proj, proj, proj, proj, proj, wg2_p, wg2_t, bg.reshape(1, hk), bg.reshape(hk, 1),
      jnp.asarray(a_all, BF16), jnp.asarray(akt, BF16), jnp.asarray(masks, F32), norm_w.reshape(1, GLA_DV))


POOL_HALO = 16


def _pool_kernel(cur_ref, prev_ref, w_ref, sc_ref, o_ref):
    tp = cur_ref.shape[1]
    j = pl.program_id(1)
    cur = cur_ref[0]
    halo = jnp.where(j > 0, prev_ref[0, tp - POOL_HALO:, :], 0.0)
    ext = jnp.concatenate([halo, cur], axis=0)
    pos = j * tp + lax.broadcasted_iota(jnp.int32, (tp, 1), 0)
    outs = []
    for g, w in enumerate(POOL_WINDOWS):
        a = ext[:, g * POOL_GROUP_DIM:(g + 1) * POOL_GROUP_DIM]
        shift = 1
        while shift < w:
            a = a + pltpu.roll(a, shift, 0)
            shift *= 2
        cnt = jnp.minimum(pos + 1, w).astype(F32)
        p = a[POOL_HALO:, :] / cnt - cur[:, g * POOL_GROUP_DIM:(g + 1) * POOL_GROUP_DIM]
        outs.append(_dot(p.astype(BF16), w_ref[g]))
    o_ref[0] = jnp.concatenate(outs, axis=1) * sc_ref[...]


def _pool(proj, pool_w, pool_scale, tp):
    b, s, _ = proj.shape
    c = BRANCH_WIDTH
    cu = _OFF["pu"] // c
    return pl.pallas_call(
        _pool_kernel,
        out_shape=jax.ShapeDtypeStruct((b, s, c), F32),
        grid=(b, s // tp),
        in_specs=[pl.BlockSpec((1, tp, c), lambda i, j: (i, j, cu)),
                  pl.BlockSpec((1, tp, c), lambda i, j: (i, jnp.maximum(j - 1, 0), cu)),
                  pl.BlockSpec((POOL_GROUPS, POOL_GROUP_DIM, POOL_GROUP_DIM), lambda i, j: (0, 0, 0)),
                  pl.BlockSpec((1, c), lambda i, j: (0, 0))],
        out_specs=pl.BlockSpec((1, tp, c), lambda i, j: (i, j, 0)),
        compiler_params=_params("parallel", "parallel"),
        name="pool",
    )(proj, proj, pool_w.astype(BF16), pool_scale.reshape(1, c))


def _t5_bucket_np(dist):
    n = np.maximum(dist, 0)
    large = REL_MAX_EXACT + (np.log(np.maximum(n, 1).astype(np.float32) / REL_MAX_EXACT)
                             / math.log(REL_MAX_DIST / REL_MAX_EXACT)
                             * (REL_BUCKETS - REL_MAX_EXACT)).astype(np.int32)
    large = np.minimum(large, REL_BUCKETS - 1)
    return np.where(n < REL_MAX_EXACT, n, large).astype(np.int32)


def _bias_kernel(rb_ref, bk_ref, o_ref):
    h = pl.program_id(0)
    for which in range(2):
        bk = bk_ref[which]
        acc = jnp.full(bk.shape, NEG_INF, F32)
        for bucket in range(REL_BUCKETS):
            acc = jnp.where(bk == bucket, rb_ref[bucket, h], acc)
        o_ref[0, which] = acc


def _bias_tables(rel_bias):
    blk = MOBA_BLOCK
    i = np.arange(blk)[:, None]
    j = np.arange(blk)[None, :]
    own = np.where(i >= j, _t5_bucket_np(i - j), -1)
    prev = _t5_bucket_np(i + blk - j)
    buckets = np.stack([own, prev]).astype(np.int32)
    return pl.pallas_call(
        _bias_kernel,
        out_shape=jax.ShapeDtypeStruct((MOBA_HEADS, 2, blk, blk), F32),
        grid=(MOBA_HEADS,),
        in_specs=[pl.BlockSpec(memory_space=pltpu.SMEM),
                  pl.BlockSpec((2, blk, blk), lambda h: (0, 0, 0))],
        out_specs=pl.BlockSpec((1, 2, blk, blk), lambda h: (h, 0, 0, 0)),
        compiler_params=_params("parallel"),
        name="moba_bias",
    )(rel_bias, jnp.asarray(buckets))


def _moba_kernel(rb_ref, q_ref, k_ref, v_ref, qn_ref, kn_ref, bias_ref, eye_ref, o_ref,
                 kaug_ref, vb_ref, kmh_ref, kml_ref):
    blk, dh = MOBA_BLOCK, MOBA_HEAD_DIM
    nb = k_ref.shape[1] // blk
    h = pl.program_id(1)
    qb = pl.program_id(2)

    @pl.when(qb == 0)
    def _():
        lane = lax.broadcasted_iota(jnp.int32, (blk, dh), 1)
        means = []
        for n in range(nb):
            kh = _rms(k_ref[0, n * blk:(n + 1) * blk, :], kn_ref[...])
            kaug_ref[n * blk:(n + 1) * blk, :dh] = kh.astype(BF16)
            kaug_ref[n * blk:(n + 1) * blk, dh:] = jnp.where(lane == n, 1.0, 0.0).astype(BF16)
            vb_ref[n * blk:(n + 1) * blk, :] = v_ref[0, n * blk:(n + 1) * blk, :].astype(BF16)
            means.append(jnp.mean(kh, axis=0, keepdims=True))
        km = jnp.concatenate(means, axis=0)
        km_hi, km_lo = _split_hi_lo(km)
        kmh_ref[...] = km_hi
        kml_ref[...] = km_lo

    qh = _rms(q_ref[0], qn_ref[...])

    q_hi, q_lo = _split_hi_lo(qh)
    sc = _dot_nt(kmh_ref[...], q_hi) + _dot_nt(kmh_ref[...], q_lo) + _dot_nt(kml_ref[...], q_hi)
    n_iota = lax.broadcasted_iota(jnp.int32, (nb, blk), 0)
    valid = n_iota < qb
    rank = jnp.zeros((nb, blk), jnp.int32)
    for m in range(nb):
        sm = sc[m:m + 1, :]
        beats = ((sm > sc) | ((sm == sc) & (m < n_iota))) & (m < qb)
        rank = rank + beats.astype(jnp.int32)
    notsel = jnp.where(valid & (rank < MOBA_TOPK), 0.0, 1.0)
    notsel_p = jnp.concatenate([notsel, jnp.zeros((dh - nb, blk), F32)], axis=0).astype(BF16)
    neg_cols = _dot_nt(eye_ref[...], notsel_p) * NEG_INF

    q_sc = (qh * (dh ** -0.5)).astype(BF16)
    q_aug = jnp.concatenate([q_sc, neg_cols.astype(BF16)], axis=1)

    s_own = _dot_nt(q_sc, kaug_ref[pl.ds(pl.multiple_of(qb * blk, blk), blk), :dh]) + bias_ref[0, 0]
    m0 = jnp.max(s_own, axis=1, keepdims=True)
    p0 = jnp.exp(s_own - m0)
    l0 = jnp.sum(p0, axis=1, keepdims=True)
    acc0 = _dot(p0.astype(BF16), vb_ref[pl.ds(pl.multiple_of(qb * blk, blk), blk), :])

    def attend(n, bias, carry):
        m_i, l_i, acc = carry
        off = pl.multiple_of(n * blk, blk)
        s = _dot_nt(q_aug, kaug_ref[pl.ds(off, blk), :]) + bias
        m_new = jnp.maximum(m_i, jnp.max(s, axis=1, keepdims=True))
        alpha = jnp.exp(m_i - m_new)
        p = jnp.exp(s - m_new)
        l_new = alpha * l_i + jnp.sum(p, axis=1, keepdims=True)
        acc_new = alpha * acc + _dot(p.astype(BF16), vb_ref[pl.ds(off, blk), :])
        return m_new, l_new, acc_new

    prev = jnp.maximum(qb - 1, 0)
    carry = lax.cond(qb > 0, lambda c: attend(prev, bias_ref[0, 1], c), lambda c: c, (m0, l0, acc0))
    far_bias = rb_ref[REL_BUCKETS - 1, h]
    carry = lax.fori_loop(0, jnp.maximum(qb - 1, 0), lambda n, c: attend(n, far_bias, c), carry)
    _, l_f, acc_f = carry
    o_ref[0] = acc_f / l_f


def _moba(proj, qn_w, kn_w, rel_bias, bias_tab):
    b, s, _ = proj.shape
    blk, dh = MOBA_BLOCK, MOBA_HEAD_DIM
    cq, ck, cv = _OFF["mq"] // dh, _OFF["mk"] // dh, _OFF["mv"] // dh
    eye = jnp.asarray(np.eye(blk, dtype=np.float32), BF16)
    return pl.pallas_call(
        _moba_kernel,
        out_shape=jax.ShapeDtypeStruct((b, s, MOBA_HEADS * dh), F32),
        grid=(b, MOBA_HEADS, s // blk),
        in_specs=[pl.BlockSpec(memory_space=pltpu.SMEM),
                  pl.BlockSpec((1, blk, dh), lambda i, h, j: (i, j, cq + h)),
                  pl.BlockSpec((1, s, dh), lambda i, h, j: (i, 0, ck + h)),
                  pl.BlockSpec((1, s, dh), lambda i, h, j: (i, 0, cv + h)),
                  pl.BlockSpec((1, dh), lambda i, h, j: (0, 0)),
                  pl.BlockSpec((1, dh), lambda i, h, j: (0, 0)),
                  pl.BlockSpec((1, 2, blk, blk), lambda i, h, j: (h, 0, 0, 0)),
                  pl.BlockSpec((blk, blk), lambda i, h, j: (0, 0))],
        out_specs=pl.BlockSpec((1, blk, dh), lambda i, h, j: (i, j, h)),
        scratch_shapes=[pltpu.VMEM((s, 2 * dh), BF16), pltpu.VMEM((s, dh), BF16),
                        pltpu.VMEM((s // blk, dh), BF16), pltpu.VMEM((s // blk, dh), BF16)],
        compiler_params=_params("parallel", "parallel", "arbitrary"),
        name="moba",
    )(rel_bias, proj, proj, proj, qn_w.reshape(1, dh), kn_w.reshape(1, dh), bias_tab, eye)


def _sigmoid(x):
    return 1.0 / (1.0 + jnp.exp(-x))


def _merge_kernel(x_ref, ya_ref, yb_ref, yc_ref, ga_ref, gb_ref, gc_ref, wa_ref, wb_ref, wc_ref, wo_ref, o_ref):
    merged = (_sigmoid(ga_ref[...]) * _dot(ya_ref[...].astype(BF16), wa_ref[...])
              + _sigmoid(gb_ref[...]) * _dot(yb_ref[...].astype(BF16), wb_ref[...])
              + _sigmoid(gc_ref[...]) * _dot(yc_ref[...].astype(BF16), wc_ref[...]))
    o_ref[...] = x_ref[...] + _dot(merged.astype(BF16), wo_ref[...])


def _merge(x2d, ya, yb, yc, proj2d, wa, wb, wc, wo, tm):
    m = x2d.shape[0]
    c = BRANCH_WIDTH
    row = lambda w, col: pl.BlockSpec((tm, w), lambda i: (i, col))
    const = lambda *shape: pl.BlockSpec(shape, lambda i: (0,) * len(shape))
    return pl.pallas_call(
        _merge_kernel,
        out_shape=jax.ShapeDtypeStruct((m, D_MODEL), F32),
        grid=(m // tm,),
        in_specs=[row(D_MODEL, 0), row(c, 0), row(c, 0), row(c, 0),
                  row(D_MODEL, _OFF["ga"] // D_MODEL), row(D_MODEL, _OFF["gb"] // D_MODEL),
                  row(D_MODEL, _OFF["gc"] // D_MODEL),
                  const(c, D_MODEL), const(c, D_MODEL), const(c, D_MODEL), const(D_MODEL, D_MODEL)],
        out_specs=row(D_MODEL, 0),
        compiler_params=_params("parallel"),
        name="merge",
    )(x2d, ya, yb, yc, proj2d, proj2d, proj2d, wa, wb, wc, wo)


def _ffn_kernel(x_ref, nw_ref, wg_ref, wu_ref, wd_ref, o_ref):
    x = x_ref[...]
    h = _rms(x, nw_ref[...]).astype(BF16)
    acc = x
    for c in range(FFN_NCHUNK):
        g = _dot(h, wg_ref[c])
        u = _dot(h, wu_ref[c])
        act = (g * _sigmoid(g) * u).astype(BF16)
        acc = acc + _dot(act, wd_ref[c])
    o_ref[...] = acc


def _ffn(x2d, norm_w, wg, wu, wd, tm):
    m = x2d.shape[0]
    const3 = lambda a, b, c: pl.BlockSpec((a, b, c), lambda i: (0, 0, 0), pipeline_mode=pl.Buffered(1))
    return pl.pallas_call(
        _ffn_kernel,
        out_shape=jax.ShapeDtypeStruct((m, D_MODEL), F32),
        grid=(m // tm,),
        in_specs=[pl.BlockSpec((tm, D_MODEL), lambda i: (i, 0)),
                  pl.BlockSpec((1, D_MODEL), lambda i: (0, 0)),
                  const3(FFN_NCHUNK, D_MODEL, FFN_CHUNK), const3(FFN_NCHUNK, D_MODEL, FFN_CHUNK),
                  const3(FFN_NCHUNK, FFN_CHUNK, D_MODEL)],
        out_specs=pl.BlockSpec((tm, D_MODEL), lambda i: (i, 0)),
        compiler_params=_params("parallel"),
        name="ffn",
    )(x2d, norm_w.reshape(1, D_MODEL), wg, wu, wd)


def _relayout_w_in(w):
    cols = [w[:, _SRC[n][0]:_SRC[n][1]] for n in _ORDER]
    used = sum(c.shape[1] for c in cols)
    cols.append(jnp.zeros((w.shape[0], PROJ_COLS - used), w.dtype))
    return jnp.concatenate(cols, axis=1).astype(BF16)


def _row_tile(m, want):
    t = min(m, want)
    assert m % t == 0
    return t


def kernel(x, norm1_w, w_in, gla_wg2, gla_bg, gla_norm_w, pool_w, pool_scale, moba_qn_w, moba_kn_w,
           rel_bias, w_up_a, w_up_b, w_up_c, w_out, norm2_w, ffn_w_gate, ffn_w_up, ffn_w_down):
    b, s, d = x.shape
    assert d == D_MODEL and s % MOBA_BLOCK == 0 and s % GLA_T == 0
    m = b * s
    bias_tab = _bias_tables(rel_bias)
    x2d = x.reshape(m, d)
    for l in range(DEPTH):
        w_p = _relayout_w_in(w_in[l])
        proj2d = _inproj(x2d, norm1_w[l], w_p, _row_tile(m, 1024))
        proj = proj2d.reshape(b, s, PROJ_COLS)
        ya = _gla(proj, gla_wg2[l], gla_bg[l], gla_norm_w[l])
        yb = _pool(proj, pool_w[l], pool_scale[l], _row_tile(s, 256))
        yc = _moba(proj, moba_qn_w[l], moba_kn_w[l], rel_bias, bias_tab)
        x2d = _merge(x2d, ya.reshape(m, -1), yb.reshape(m, -1), yc.reshape(m, -1), proj2d,
                     w_up_a[l].astype(BF16), w_up_b[l].astype(BF16), w_up_c[l].astype(BF16),
                     w_out[l].astype(BF16), _row_tile(m, 512))
        wg = ffn_w_gate[l].astype(BF16).reshape(D_MODEL, FFN_NCHUNK, FFN_CHUNK).transpose(1, 0, 2)
        wu = ffn_w_up[l].astype(BF16).reshape(D_MODEL, FFN_NCHUNK, FFN_CHUNK).transpose(1, 0, 2)
        wd = ffn_w_down[l].astype(BF16).reshape(FFN_NCHUNK, FFN_CHUNK, D_MODEL)
        x2d = _ffn(x2d, norm2_w[l], wg, wu, wd, _row_tile(m, 512))
    return x2d.reshape(b, s, d)
```

```python
import functools
import math

import numpy as np
import jax
import jax.numpy as jnp
from jax import lax
from jax.experimental import pallas as pl
from jax.experimental.pallas import tpu as pltpu

D_MODEL = 1024
DEPTH = 2
BRANCH_WIDTH = D_MODEL // 2
GLA_HEADS = 4
GLA_DV = BRANCH_WIDTH // GLA_HEADS
GLA_DK = GLA_DV // 2
GLA_LOW_RANK = 16
GLA_GATE_NORMALIZER = 16.0
POOL_WINDOWS = (2, 4, 8, 16)
POOL_GROUPS = 4
POOL_GROUP_DIM = BRANCH_WIDTH // POOL_GROUPS
MOBA_HEADS = 4
MOBA_HEAD_DIM = BRANCH_WIDTH // MOBA_HEADS
MOBA_BLOCK = 256
MOBA_TOPK = 3
REL_BUCKETS = 32
REL_MAX_EXACT = REL_BUCKETS // 2
REL_MAX_DIST = 128
D_FF = ((8 * D_MODEL + 767) // 768) * 256
NORM_EPS = 1e-6
NEG_INF = -1e30

LANES = 128
BF16_SUBLANES = 16
VMEM_LIMIT_BYTES = 56 * 1024 * 1024

_SRC = dict(gq=(0, 256), gk=(256, 512), gv=(512, 1024), glr=(1024, 1040), gr=(1040, 1552), pu=(1552, 2064),
            mq=(2064, 2576), mk=(2576, 3088), mv=(3088, 3600), ga=(3600, 4624), gb=(4624, 5648), gc=(5648, 6672))
_ORDER = ("ga", "gb", "gc", "gv", "gr", "pu", "mq", "mk", "mv", "gq", "gk", "glr")
_WIDTH = dict(ga=1024, gb=1024, gc=1024, gv=512, gr=512, pu=512, mq=512, mk=512, mv=512, gq=256, gk=256, glr=LANES)
_OFF = {}
_o = 0
for _n in _ORDER:
    assert _o % _WIDTH[_n] == 0
    _OFF[_n] = _o
    _o += _WIDTH[_n]
PROJ_TN = 768
PROJ_COLS = -(-_o // PROJ_TN) * PROJ_TN

GLA_T = 128
GLA_LEVELS = 7
GLA_BATCH_PER_STEP = 4
POOL_HALO = 16
FFN_CHUNK = 256
FFN_NCHUNK = D_FF // FFN_CHUNK

F32 = jnp.float32
BF16 = jnp.bfloat16


def _params(*sem):
    return pltpu.CompilerParams(dimension_semantics=sem, vmem_limit_bytes=VMEM_LIMIT_BYTES)


def _dot(a, b):
    return jnp.dot(a, b, preferred_element_type=F32)


def _dot_nt(a, b):
    return lax.dot_general(a, b, (((1,), (1,)), ((), ())), preferred_element_type=F32)


def _split_hi_lo(x):
    hi = x.astype(BF16)
    lo = (x - hi.astype(F32)).astype(BF16)
    return hi, lo


def _rms(x, w):
    return x * lax.rsqrt(jnp.mean(x * x, axis=-1, keepdims=True) + NORM_EPS) * w


def _sigmoid(x):
    return 1.0 / (1.0 + jnp.exp(-x))


def _inproj_kernel(x_ref, nw_ref, w_ref, o_ref, h_ref):
    @pl.when(pl.program_id(1) == 0)
    def _():
        h_ref[...] = _rms(x_ref[...], nw_ref[...]).astype(BF16)

    o_ref[...] = _dot(h_ref[...], w_ref[...]).astype(BF16)


def _inproj(x2d, norm_w, w_p, tm):
    m = x2d.shape[0]
    return pl.pallas_call(
        _inproj_kernel,
        out_shape=jax.ShapeDtypeStruct((m, PROJ_COLS), BF16),
        grid=(m // tm, PROJ_COLS // PROJ_TN),
        in_specs=[pl.BlockSpec((tm, D_MODEL), lambda i, j: (i, 0)),
                  pl.BlockSpec((1, D_MODEL), lambda i, j: (0, 0)),
                  pl.BlockSpec((D_MODEL, PROJ_TN), lambda i, j: (0, j))],
        out_specs=pl.BlockSpec((tm, PROJ_TN), lambda i, j: (i, j)),
        scratch_shapes=[pltpu.VMEM((tm, D_MODEL), BF16)],
        compiler_params=_params("parallel", "arbitrary"),
        name="inproj",
    )(x2d, norm_w.reshape(1, D_MODEL), w_p)


def _gla_constants():
    t = np.arange(GLA_T)
    u = np.arange(GLA_T)
    a_rows = []
    level = np.full((GLA_T, GLA_T), -1, np.int32)
    for lev in range(GLA_LEVELS):
        s = (GLA_T // 2) >> lev
        g = 2 * s
        first = (t % g) < s
        ref = (t // g) * g + s - 1
        a = np.where(first[:, None],
                     (u[None, :] > t[:, None]) & (u[None, :] <= ref[:, None]),
                     (u[None, :] > ref[:, None]) & (u[None, :] <= t[:, None]))
        a_rows.append(a)
        same = (t[:, None] // g) == (t[None, :] // g)
        level[same & (~first)[:, None] & first[None, :]] = lev
    a_rows.append(u[None, :] <= t[:, None])
    level[np.eye(GLA_T, dtype=bool)] = GLA_LEVELS
    a_all = np.concatenate(a_rows, axis=0).astype(np.float32)
    akt = np.concatenate([(u[:, None] > t[None, :]), np.ones((GLA_T, GLA_T), bool)], axis=1).astype(np.float32)
    return np.concatenate([a_all, a_all], axis=1), np.concatenate([akt, akt], axis=0), level


def _log_sigmoid(z):
    return jnp.minimum(z, 0.0) - jnp.log1p(jnp.exp(-jnp.abs(z)))


def _gla_kernel(q_ref, k_ref, v_ref, r_ref, glr_ref, wg2_ref, bg_ref, a_ref, akt_ref, lvl_ref, nw_ref, o_ref,
                s_ref, sbd_ref):
    nh, dk, dv, t = GLA_HEADS, GLA_DK, GLA_DV, GLA_T

    @pl.when(pl.program_id(1) == 0)
    def _():
        s_ref[...] = jnp.zeros_like(s_ref)
        sbd_ref[...] = jnp.zeros_like(sbd_ref)

    rows = range(q_ref.shape[0])
    each = lambda f: [f(bi) for bi in rows]
    q8 = each(lambda bi: q_ref[bi] * (dk ** -0.5))
    k = each(lambda bi: k_ref[bi])
    v = each(lambda bi: v_ref[bi])
    s_old = each(lambda bi: [s_ref[bi, h] for h in range(nh)])

    inv = 1.0 / GLA_GATE_NORMALIZER
    glog = each(lambda bi: _log_sigmoid(_dot(glr_ref[bi], wg2_ref[...]) + bg_ref[...]) * inv)
    glog_t = each(lambda bi: glog[bi].T)

    e_all = each(lambda bi: _dot(a_ref[...], jnp.concatenate(_split_hi_lo(glog[bi]), axis=0)))
    et_all = each(lambda bi: _dot(jnp.concatenate(_split_hi_lo(glog_t[bi]), axis=1), akt_ref[...]))

    qe = each(lambda bi: q8[bi] * jnp.exp(e_all[bi][GLA_LEVELS * t:(GLA_LEVELS + 1) * t]).astype(BF16))
    o_inter = each(lambda bi: _dot(qe[bi], sbd_ref[bi]))

    lane_head = lax.broadcasted_iota(jnp.int32, (t, nh * dk), 1) // dk
    head_masks = [lane_head == h for h in range(nh)]
    lvl = lvl_ref[...]
    attn = each(lambda bi: [jnp.zeros((t, t), F32) for _ in range(nh)])
    for lev in range(GLA_LEVELS + 1):
        m = lvl == lev
        for bi in rows:
            if lev < GLA_LEVELS:
                e = jnp.exp(e_all[bi][lev * t:(lev + 1) * t]).astype(BF16)
                ql = q8[bi] * e
                kl = k[bi] * e
            else:
                ql = q8[bi]
                kl = k[bi]
            kbd = jnp.concatenate([jnp.where(head_masks[h], kl, jnp.zeros_like(kl)) for h in range(nh)], axis=0)
            sc = _dot_nt(ql, kbd)
            for h in range(nh):
                attn[bi][h] = jnp.where(m, sc[:, h * t:(h + 1) * t], attn[bi][h])

    nw = nw_ref[...]
    for bi in rows:
        outs = []
        for h in range(nh):
            cols = slice(h * dv, (h + 1) * dv)
            outs.append(_rms(_dot(attn[bi][h].astype(BF16), v[bi][:, cols]) + o_inter[bi][:, cols], nw))
        r = r_ref[bi].astype(F32)
        o_ref[bi] = (jnp.concatenate(outs, axis=1) * (r * _sigmoid(r))).astype(BF16)

    k_tail_t = each(lambda bi: (k[bi].astype(F32).T * jnp.exp(et_all[bi][:, :t])).astype(BF16))
    dec_t = each(lambda bi: jnp.exp(et_all[bi][:, t:]))
    for bi in rows:
        for h in range(nh):
            ch = slice(h * dk, (h + 1) * dk)
            s_new = dec_t[bi][ch, :] * s_old[bi][h] + _dot(k_tail_t[bi][ch, :], v[bi][:, h * dv:(h + 1) * dv])
            s_ref[bi, h] = s_new
            sbd_ref[bi, ch, h * dv:(h + 1) * dv] = s_new.astype(BF16)


def _gla(proj, wg2, bg, norm_w):
    b, s, _ = proj.shape
    t = GLA_T
    hk = GLA_HEADS * GLA_DK
    hv = GLA_HEADS * GLA_DV
    a_all, akt, level = _gla_constants()
    wg2_p = jnp.zeros((LANES, hk), F32).at[:GLA_LOW_RANK].set(wg2).astype(BF16)
    const = lambda *shape: pl.BlockSpec(shape, lambda i, j: (0,) * len(shape))
    nb = GLA_BATCH_PER_STEP if b % GLA_BATCH_PER_STEP == 0 else 1
    return pl.pallas_call(
        _gla_kernel,
        out_shape=jax.ShapeDtypeStruct((b, s, hv), BF16),
        grid=(b // nb, s // t),
        in_specs=[pl.BlockSpec((nb, t, hk), lambda i, j: (i, j, _OFF["gq"] // hk)),
                  pl.BlockSpec((nb, t, hk), lambda i, j: (i, j, _OFF["gk"] // hk)),
                  pl.BlockSpec((nb, t, hv), lambda i, j: (i, j, _OFF["gv"] // hv)),
                  pl.BlockSpec((nb, t, hv), lambda i, j: (i, j, _OFF["gr"] // hv)),
                  pl.BlockSpec((nb, t, LANES), lambda i, j: (i, j, _OFF["glr"] // LANES)),
                  const(LANES, hk), const(1, hk),
                  const((GLA_LEVELS + 1) * t, 2 * t), const(2 * t, 2 * t), const(t, t),
                  const(1, GLA_DV)],
        out_specs=pl.BlockSpec((nb, t, hv), lambda i, j: (i, j, 0)),
        scratch_shapes=[pltpu.VMEM((nb, GLA_HEADS, GLA_DK, GLA_DV), F32), pltpu.VMEM((nb, hk, hv), BF16)],
        compiler_params=_params("parallel", "arbitrary"),
        name="gla",
    )(proj, proj, proj, proj, proj, wg2_p, bg.reshape(1, hk),
      jnp.asarray(a_all, BF16), jnp.asarray(akt, BF16), jnp.asarray(level), norm_w.reshape(1, GLA_DV))


def _t5_bucket_np(dist):
    n = np.maximum(dist, 0)
    large = REL_MAX_EXACT + (np.log(np.maximum(n, 1).astype(np.float32) / REL_MAX_EXACT)
                             / math.log(REL_MAX_DIST / REL_MAX_EXACT)
                             * (REL_BUCKETS - REL_MAX_EXACT)).astype(np.int32)
    large = np.minimum(large, REL_BUCKETS - 1)
    return np.where(n < REL_MAX_EXACT, n, large).astype(np.int32)


def _bias_kernel(rb_ref, bk_ref, o_ref):
    h = pl.program_id(0)
    for which in range(2):
        bk = bk_ref[which]
        acc = jnp.full(bk.shape, NEG_INF, F32)
        for bucket in range(REL_BUCKETS):
            acc = jnp.where(bk == bucket, rb_ref[bucket, h], acc)
        o_ref[0, which] = acc


def _bias_tables(rel_bias):
    blk = MOBA_BLOCK
    i = np.arange(blk)[:, None]
    j = np.arange(blk)[None, :]
    own = np.where(i >= j, _t5_bucket_np(i - j), -1)
    prev = _t5_bucket_np(i + blk - j)
    buckets = np.stack([own, prev]).astype(np.int32)
    return pl.pallas_call(
        _bias_kernel,
        out_shape=jax.ShapeDtypeStruct((MOBA_HEADS, 2, blk, blk), F32),
        grid=(MOBA_HEADS,),
        in_specs=[pl.BlockSpec(memory_space=pltpu.SMEM),
                  pl.BlockSpec((2, blk, blk), lambda h: (0, 0, 0))],
        out_specs=pl.BlockSpec((1, 2, blk, blk), lambda h: (h, 0, 0, 0)),
        compiler_params=_params("parallel"),
        name="moba_bias",
    )(rel_bias, jnp.asarray(buckets))


def _moba_kernel(rb_ref, q_ref, k_ref, v_ref, qn_ref, kn_ref, bias_ref, eye_ref, o_ref, kaug_ref, kmh_ref, kml_ref):
    blk, dh, nh = MOBA_BLOCK, MOBA_HEAD_DIM, MOBA_HEADS
    nb = k_ref.shape[1] // blk
    qb = pl.program_id(1)

    @pl.when(qb == 0)
    def _():
        lane = lax.broadcasted_iota(jnp.int32, (blk, dh), 1)
        for h in range(nh):
            means = []
            for n in range(nb):
                rows = slice(n * blk, (n + 1) * blk)
                kh = _rms(k_ref[0, rows, h * dh:(h + 1) * dh].astype(F32), kn_ref[...])
                kaug_ref[h, rows, :dh] = kh.astype(BF16)
                kaug_ref[h, rows, dh:] = jnp.where(lane == n, 1.0, 0.0).astype(BF16)
                means.append(jnp.mean(kh, axis=0, keepdims=True))
            km_hi, km_lo = _split_hi_lo(jnp.concatenate(means, axis=0))
            kmh_ref[h] = km_hi
            kml_ref[h] = km_lo

    own = pl.ds(pl.multiple_of(qb * blk, blk), blk)
    n_iota = lax.broadcasted_iota(jnp.int32, (nb, blk), 0)
    q_augs, carry = [], []
    for h in range(nh):
        cols = slice(h * dh, (h + 1) * dh)
        qh = _rms(q_ref[0, :, cols].astype(F32), qn_ref[...])
        q_hi, q_lo = _split_hi_lo(qh)
        sc = _dot_nt(kmh_ref[h], q_hi) + _dot_nt(kmh_ref[h], q_lo) + _dot_nt(kml_ref[h], q_hi)
        rank = jnp.zeros((nb, blk), jnp.int32)
        for m in range(nb):
            sm = sc[m:m + 1, :]
            beats = ((sm > sc) | ((sm == sc) & (m < n_iota))) & (m < qb)
            rank = rank + beats.astype(jnp.int32)
        notsel = jnp.where((n_iota < qb) & (rank < MOBA_TOPK), 0.0, 1.0)
        notsel_p = jnp.concatenate([notsel, jnp.zeros((dh - nb, blk), F32)], axis=0).astype(BF16)
        neg_cols = _dot_nt(eye_ref[...], notsel_p) * NEG_INF
        q_sc = (qh * (dh ** -0.5)).astype(BF16)
        q_augs.append(jnp.concatenate([q_sc, neg_cols.astype(BF16)], axis=1))

        s_own = _dot_nt(q_sc, kaug_ref[h, own, :dh]) + bias_ref[h, 0]
        m0 = jnp.max(s_own, axis=1, keepdims=True)
        p0 = jnp.exp(s_own - m0)
        l0 = jnp.sum(p0, axis=1, keepdims=True)
        carry.append((m0, l0, _dot(p0.astype(BF16), v_ref[0, own, cols])))

    def attend(n, biases, carry):
        rows = pl.ds(pl.multiple_of(n * blk, blk), blk)
        out = []
        for h in range(nh):
            m_i, l_i, acc = carry[h]
            s = _dot_nt(q_augs[h], kaug_ref[h, rows, :]) + biases[h]
            m_new = jnp.maximum(m_i, jnp.max(s, axis=1, keepdims=True))
            alpha = jnp.exp(m_i - m_new)
            p = jnp.exp(s - m_new)
            l_new = alpha * l_i + jnp.sum(p, axis=1, keepdims=True)
            acc_new = alpha * acc + _dot(p.astype(BF16), v_ref[0, rows, h * dh:(h + 1) * dh])
            out.append((m_new, l_new, acc_new))
        return out

    prev = jnp.maximum(qb - 1, 0)
    carry = lax.cond(qb > 0, lambda c: attend(prev, [bias_ref[h, 1] for h in range(nh)], c), lambda c: c, carry)
    far = [rb_ref[REL_BUCKETS - 1, h] for h in range(nh)]
    carry = lax.fori_loop(0, jnp.maximum(qb - 1, 0), lambda n, c: attend(n, far, c), carry)
    o_ref[0] = jnp.concatenate([acc / l for (_, l, acc) in carry], axis=1).astype(BF16)


def _moba(proj, qn_w, kn_w, rel_bias, bias_tab):
    b, s, _ = proj.shape
    blk, dh, nh = MOBA_BLOCK, MOBA_HEAD_DIM, MOBA_HEADS
    c = nh * dh
    eye = jnp.asarray(np.eye(blk, dtype=np.float32), BF16)
    return pl.pallas_call(
        _moba_kernel,
        out_shape=jax.ShapeDtypeStruct((b, s, c), BF16),
        grid=(b, s // blk),
        in_specs=[pl.BlockSpec(memory_space=pltpu.SMEM),
                  pl.BlockSpec((1, blk, c), lambda i, j: (i, j, _OFF["mq"] // c)),
                  pl.BlockSpec((1, s, c), lambda i, j: (i, 0, _OFF["mk"] // c)),
                  pl.BlockSpec((1, s, c), lambda i, j: (i, 0, _OFF["mv"] // c)),
                  pl.BlockSpec((1, dh), lambda i, j: (0, 0)),
                  pl.BlockSpec((1, dh), lambda i, j: (0, 0)),
                  pl.BlockSpec((nh, 2, blk, blk), lambda i, j: (0, 0, 0, 0)),
                  pl.BlockSpec((blk, blk), lambda i, j: (0, 0))],
        out_specs=pl.BlockSpec((1, blk, c), lambda i, j: (i, j, 0)),
        scratch_shapes=[pltpu.VMEM((nh, s, 2 * dh), BF16),
                        pltpu.VMEM((nh, s // blk, dh), BF16), pltpu.VMEM((nh, s // blk, dh), BF16)],
        compiler_params=_params("parallel", "arbitrary"),
        name="moba",
    )(rel_bias, proj, proj, proj, qn_w.reshape(1, dh), kn_w.reshape(1, dh), bias_tab, eye)


def _pool_branch(u_ref, halo_ref, pw_ref, ps_ref, start):
    tm = u_ref.shape[0]
    cur = u_ref[...].astype(F32)
    halo = jnp.where(start > 0, halo_ref[...].astype(F32), 0.0)
    ext = jnp.concatenate([halo, cur], axis=0)
    pos = start + lax.broadcasted_iota(jnp.int32, (tm, 1), 0)
    outs = []
    for g, w in enumerate(POOL_WINDOWS):
        cols = slice(g * POOL_GROUP_DIM, (g + 1) * POOL_GROUP_DIM)
        a = ext[:, cols]
        shift = 1
        while shift < w:
            a = a + pltpu.roll(a, shift, 0)
            shift *= 2
        cnt = jnp.minimum(pos + 1, w).astype(F32)
        p = a[POOL_HALO:, :] / cnt - cur[:, cols]
        outs.append(_dot(p.astype(BF16), pw_ref[g]))
    return jnp.concatenate(outs, axis=1) * ps_ref[...]


def _mix_kernel(seq, x_ref, ya_ref, yc_ref, u_ref, halo_ref, ga_ref, gb_ref, gc_ref, pw_ref, ps_ref,
                wa_ref, wb_ref, wc_ref, wo_ref, n2_ref, wg_ref, wu_ref, wd_ref, o_ref):
    tm = x_ref.shape[0]
    start = (pl.program_id(0) * tm) % seq
    yb = _pool_branch(u_ref, halo_ref, pw_ref, ps_ref, start)
    merged = (_sigmoid(ga_ref[...].astype(F32)) * _dot(ya_ref[...], wa_ref[...])
              + _sigmoid(gb_ref[...].astype(F32)) * _dot(yb.astype(BF16), wb_ref[...])
              + _sigmoid(gc_ref[...].astype(F32)) * _dot(yc_ref[...], wc_ref[...]))
    x1 = x_ref[...] + _dot(merged.astype(BF16), wo_ref[...])
    h = _rms(x1, n2_ref[...]).astype(BF16)
    acc = x1
    for c in range(FFN_NCHUNK):
        cols = slice(c * FFN_CHUNK, (c + 1) * FFN_CHUNK)
        g = _dot(h, wg_ref[:, cols])
        u = _dot(h, wu_ref[:, cols])
        act = (g * _sigmoid(g) * u).astype(BF16)
        acc = acc + _dot(act, wd_ref[cols, :])
    o_ref[...] = acc


def _mix(x2d, ya, yc, proj2d, seq, pool_w, pool_scale, wa, wb, wc, wo, norm2_w, wg, wu, wd, tm):
    m = x2d.shape[0]
    c = BRANCH_WIDTH
    assert seq % tm == 0 and tm % POOL_HALO == 0 and POOL_HALO == BF16_SUBLANES
    row = lambda w, col: pl.BlockSpec((tm, w), lambda i: (i, col))
    resident = lambda *shape: pl.BlockSpec(shape, lambda i: (0,) * len(shape), pipeline_mode=pl.Buffered(1))
    halo_blocks = tm // POOL_HALO
    return pl.pallas_call(
        functools.partial(_mix_kernel, seq),
        out_shape=jax.ShapeDtypeStruct((m, D_MODEL), F32),
        grid=(m // tm,),
        in_specs=[row(D_MODEL, 0), row(c, 0), row(c, 0), row(c, _OFF["pu"] // c),
                  pl.BlockSpec((POOL_HALO, c), lambda i: (jnp.maximum(i * halo_blocks - 1, 0), _OFF["pu"] // c)),
                  row(D_MODEL, _OFF["ga"] // D_MODEL), row(D_MODEL, _OFF["gb"] // D_MODEL),
                  row(D_MODEL, _OFF["gc"] // D_MODEL),
                  resident(POOL_GROUPS, POOL_GROUP_DIM, POOL_GROUP_DIM), resident(1, c),
                  resident(c, D_MODEL), resident(c, D_MODEL), resident(c, D_MODEL), resident(D_MODEL, D_MODEL),
                  resident(1, D_MODEL), resident(D_MODEL, D_FF), resident(D_MODEL, D_FF), resident(D_FF, D_MODEL)],
        out_specs=row(D_MODEL, 0),
        compiler_params=_params("parallel"),
        name="mix",
    )(x2d, ya, yc, proj2d, proj2d, proj2d, proj2d, proj2d, pool_w, pool_scale.reshape(1, c),
      wa, wb, wc, wo, norm2_w.reshape(1, D_MODEL), wg, wu, wd)


def _relayout_w_in(w):
    w = w.astype(BF16)
    cols = [w[:, _SRC[n][0]:_SRC[n][1]] for n in _ORDER]
    used = sum(c.shape[1] for c in cols)
    cols.append(jnp.zeros((w.shape[0], PROJ_COLS - used), BF16))
    return jnp.concatenate(cols, axis=1)


def _row_tile(m, want):
    t = min(m, want)
    assert m % t == 0
    return t


def kernel(x, norm1_w, w_in, gla_wg2, gla_bg, gla_norm_w, pool_w, pool_scale, moba_qn_w, moba_kn_w,
           rel_bias, w_up_a, w_up_b, w_up_c, w_out, norm2_w, ffn_w_gate, ffn_w_up, ffn_w_down):
    b, s, d = x.shape
    assert d == D_MODEL and s % MOBA_BLOCK == 0 and s % GLA_T == 0
    m = b * s
    bf = lambda a: a.astype(BF16)
    bias_tab = _bias_tables(rel_bias)
    x2d = x.reshape(m, d)
    for l in range(DEPTH):
        proj2d = _inproj(x2d, norm1_w[l], _relayout_w_in(w_in[l]), _row_tile(m, 1024))
        proj = proj2d.reshape(b, s, PROJ_COLS)
        ya = _gla(proj, gla_wg2[l], gla_bg[l], gla_norm_w[l])
        yc = _moba(proj, moba_qn_w[l], moba_kn_w[l], rel_bias, bias_tab)
        x2d = _mix(x2d, ya.reshape(m, -1), yc.reshape(m, -1), proj2d, s, bf(pool_w[l]), pool_scale[l],
                   bf(w_up_a[l]), bf(w_up_b[l]), bf(w_up_c[l]), bf(w_out[l]), norm2_w[l],
                   bf(ffn_w_gate[l]), bf(ffn_w_up[l]), bf(ffn_w_down[l]), _row_tile(s, 512))
    return x2d.reshape(b, s, d)
```

```python
import functools
import math

import numpy as np
import jax
import jax.numpy as jnp
from jax import lax
from jax.experimental import pallas as pl
from jax.experimental.pallas import tpu as pltpu

D_MODEL = 1024
DEPTH = 2
BRANCH_WIDTH = D_MODEL // 2
GLA_HEADS = 4
GLA_DV = BRANCH_WIDTH // GLA_HEADS
GLA_DK = GLA_DV // 2
GLA_LOW_RANK = 16
GLA_GATE_NORMALIZER = 16.0
POOL_WINDOWS = (2, 4, 8, 16)
POOL_GROUPS = 4
POOL_GROUP_DIM = BRANCH_WIDTH // POOL_GROUPS
MOBA_HEADS = 4
MOBA_HEAD_DIM = BRANCH_WIDTH // MOBA_HEADS
MOBA_BLOCK = 256
MOBA_TOPK = 3
REL_BUCKETS = 32
REL_MAX_EXACT = REL_BUCKETS // 2
REL_MAX_DIST = 128
D_FF = ((8 * D_MODEL + 767) // 768) * 256
NORM_EPS = 1e-6
NEG_INF = -1e30

LANES = 128
BF16_SUBLANES = 16
VMEM_LIMIT_BYTES = 56 * 1024 * 1024

_SECTIONS = (("gq", 256), ("gk", 256), ("gv", 512), ("glr", GLA_LOW_RANK), ("gr", 512), ("pu", 512),
             ("mq", 512), ("mk", 512), ("mv", 512), ("ga", 1024), ("gb", 1024), ("gc", 1024))
_OFF = {}
_o = 0
for _n, _w in _SECTIONS:
    if _n != "glr":
        assert _o % min(_w, 512) == 0
        _OFF[_n] = _o
        _o += _w
_OFF["glr"] = _o
PROJ_COLS = _o + LANES
GLR_SRC = sum(w for n, w in _SECTIONS[:3])
PROJ_CHUNK = 512

GLA_T = 128
GLA_LEVELS = 7
GLA_BATCH_PER_STEP = 4
POOL_HALO = 16
FFN_CHUNK = 256
FFN_NCHUNK = D_FF // FFN_CHUNK

F32 = jnp.float32
BF16 = jnp.bfloat16


def _params(*sem):
    return pltpu.CompilerParams(dimension_semantics=sem, vmem_limit_bytes=VMEM_LIMIT_BYTES)


def _dot(a, b):
    return jnp.dot(a, b, preferred_element_type=F32)


def _dot_nt(a, b):
    return lax.dot_general(a, b, (((1,), (1,)), ((), ())), preferred_element_type=F32)


def _split_hi_lo(x):
    hi = x.astype(BF16)
    lo = (x - hi.astype(F32)).astype(BF16)
    return hi, lo


def _rms(x, w):
    return x * lax.rsqrt(jnp.mean(x * x, axis=-1, keepdims=True) + NORM_EPS) * w


def _sigmoid(x):
    return 1.0 / (1.0 + jnp.exp(-x))


def _inproj_kernel(x_ref, nw_ref, w_ref, o_ref):
    h = _rms(x_ref[...], nw_ref[...]).astype(BF16)
    for lo in range(0, PROJ_COLS, PROJ_CHUNK):
        cols = slice(lo, min(lo + PROJ_CHUNK, PROJ_COLS))
        o_ref[:, cols] = _dot(h, w_ref[:, cols]).astype(BF16)


def _inproj(x2d, norm_w, w_p, layer, tm):
    m = x2d.shape[0]
    return pl.pallas_call(
        _inproj_kernel,
        out_shape=jax.ShapeDtypeStruct((m, PROJ_COLS), BF16),
        grid=(m // tm,),
        in_specs=[pl.BlockSpec((tm, D_MODEL), lambda i: (i, 0)),
                  pl.BlockSpec((None, 1, D_MODEL), lambda i: (layer, 0, 0)),
                  pl.BlockSpec((None, D_MODEL, PROJ_COLS), lambda i: (layer, 0, 0), pipeline_mode=pl.Buffered(1))],
        out_specs=pl.BlockSpec((tm, PROJ_COLS), lambda i: (i, 0)),
        compiler_params=_params("parallel"),
        name="inproj",
    )(x2d, norm_w, w_p)


def _gla_constants():
    t = np.arange(GLA_T)
    u = np.arange(GLA_T)
    a_rows = []
    level = np.full((GLA_T, GLA_T), -1, np.int32)
    for lev in range(GLA_LEVELS):
        s = (GLA_T // 2) >> lev
        g = 2 * s
        first = (t % g) < s
        ref = (t // g) * g + s - 1
        a = np.where(first[:, None],
                     (u[None, :] > t[:, None]) & (u[None, :] <= ref[:, None]),
                     (u[None, :] > ref[:, None]) & (u[None, :] <= t[:, None]))
        a_rows.append(a)
        same = (t[:, None] // g) == (t[None, :] // g)
        level[same & (~first)[:, None] & first[None, :]] = lev
    a_rows.append(u[None, :] <= t[:, None])
    level[np.eye(GLA_T, dtype=bool)] = GLA_LEVELS
    a_all = np.concatenate(a_rows, axis=0).astype(np.float32)
    akt = np.concatenate([(u[:, None] > t[None, :]), np.ones((GLA_T, GLA_T), bool)], axis=1).astype(np.float32)
    return np.concatenate([a_all, a_all], axis=1), np.concatenate([akt, akt], axis=0), level


def _log_sigmoid(z):
    return jnp.minimum(z, 0.0) - jnp.log1p(jnp.exp(-jnp.abs(z)))


def _gla_kernel(q_ref, k_ref, v_ref, r_ref, glr_ref, wg2_ref, bg_ref, a_ref, akt_ref, lvl_ref, nw_ref, o_ref,
                s_ref, sbd_ref):
    nh, dk, dv, t = GLA_HEADS, GLA_DK, GLA_DV, GLA_T

    @pl.when(pl.program_id(1) == 0)
    def _():
        s_ref[...] = jnp.zeros_like(s_ref)
        sbd_ref[...] = jnp.zeros_like(sbd_ref)

    rows = range(q_ref.shape[0])
    each = lambda f: [f(bi) for bi in rows]
    q8 = each(lambda bi: q_ref[bi] * (dk ** -0.5))
    k = each(lambda bi: k_ref[bi])
    v = each(lambda bi: v_ref[bi])
    s_old = each(lambda bi: [s_ref[bi, h] for h in range(nh)])

    inv = 1.0 / GLA_GATE_NORMALIZER
    glog = each(lambda bi: _log_sigmoid(_dot(glr_ref[bi], wg2_ref[...]) + bg_ref[...]) * inv)
    glog_t = each(lambda bi: glog[bi].T)

    e_all = each(lambda bi: _dot(a_ref[...], jnp.concatenate(_split_hi_lo(glog[bi]), axis=0)))
    et_all = each(lambda bi: _dot(jnp.concatenate(_split_hi_lo(glog_t[bi]), axis=1), akt_ref[...]))

    qe = each(lambda bi: q8[bi] * jnp.exp(e_all[bi][GLA_LEVELS * t:(GLA_LEVELS + 1) * t]).astype(BF16))
    o_inter = each(lambda bi: _dot(qe[bi], sbd_ref[bi]))

    lane_head = lax.broadcasted_iota(jnp.int32, (t, nh * dk), 1) // dk
    head_masks = [lane_head == h for h in range(nh)]
    lvl = lvl_ref[...]
    attn = each(lambda bi: [jnp.zeros((t, t), F32) for _ in range(nh)])
    for lev in range(GLA_LEVELS + 1):
        m = lvl == lev
        for bi in rows:
            if lev < GLA_LEVELS:
                e = jnp.exp(e_all[bi][lev * t:(lev + 1) * t]).astype(BF16)
                ql = q8[bi] * e
                kl = k[bi] * e
            else:
                ql = q8[bi]
                kl = k[bi]
            kbd = jnp.concatenate([jnp.where(head_masks[h], kl, jnp.zeros_like(kl)) for h in range(nh)], axis=0)
            sc = _dot_nt(ql, kbd)
            for h in range(nh):
                attn[bi][h] = jnp.where(m, sc[:, h * t:(h + 1) * t], attn[bi][h])

    nw = nw_ref[...]
    for bi in rows:
        outs = []
        for h in range(nh):
            cols = slice(h * dv, (h + 1) * dv)
            outs.append(_rms(_dot(attn[bi][h].astype(BF16), v[bi][:, cols]) + o_inter[bi][:, cols], nw))
        r = r_ref[bi].astype(F32)
        o_ref[bi] = (jnp.concatenate(outs, axis=1) * (r * _sigmoid(r))).astype(BF16)

    k_tail_t = each(lambda bi: (k[bi].astype(F32).T * jnp.exp(et_all[bi][:, :t])).astype(BF16))
    dec_t = each(lambda bi: jnp.exp(et_all[bi][:, t:]))
    for bi in rows:
        for h in range(nh):
            ch = slice(h * dk, (h + 1) * dk)
            s_new = dec_t[bi][ch, :] * s_old[bi][h] + _dot(k_tail_t[bi][ch, :], v[bi][:, h * dv:(h + 1) * dv])
            s_ref[bi, h] = s_new
            sbd_ref[bi, ch, h * dv:(h + 1) * dv] = s_new.astype(BF16)


def _gla(proj, wg2, bg, norm_w):
    b, s, _ = proj.shape
    t = GLA_T
    hk = GLA_HEADS * GLA_DK
    hv = GLA_HEADS * GLA_DV
    a_all, akt, level = _gla_constants()
    wg2_p = jnp.zeros((LANES, hk), F32).at[:GLA_LOW_RANK].set(wg2).astype(BF16)
    const = lambda *shape: pl.BlockSpec(shape, lambda i, j: (0,) * len(shape))
    nb = GLA_BATCH_PER_STEP if b % GLA_BATCH_PER_STEP == 0 else 1
    return pl.pallas_call(
        _gla_kernel,
        out_shape=jax.ShapeDtypeStruct((b, s, hv), BF16),
        grid=(b // nb, s // t),
        in_specs=[pl.BlockSpec((nb, t, hk), lambda i, j: (i, j, _OFF["gq"] // hk)),
                  pl.BlockSpec((nb, t, hk), lambda i, j: (i, j, _OFF["gk"] // hk)),
                  pl.BlockSpec((nb, t, hv), lambda i, j: (i, j, _OFF["gv"] // hv)),
                  pl.BlockSpec((nb, t, hv), lambda i, j: (i, j, _OFF["gr"] // hv)),
                  pl.BlockSpec((nb, t, LANES), lambda i, j: (i, j, _OFF["glr"] // LANES)),
                  const(LANES, hk), const(1, hk),
                  const((GLA_LEVELS + 1) * t, 2 * t), const(2 * t, 2 * t), const(t, t),
                  const(1, GLA_DV)],
        out_specs=pl.BlockSpec((nb, t, hv), lambda i, j: (i, j, 0)),
        scratch_shapes=[pltpu.VMEM((nb, GLA_HEADS, GLA_DK, GLA_DV), F32), pltpu.VMEM((nb, hk, hv), BF16)],
        compiler_params=_params("parallel", "arbitrary"),
        name="gla",
    )(proj, proj, proj, proj, proj, wg2_p, bg.reshape(1, hk),
      jnp.asarray(a_all, BF16), jnp.asarray(akt, BF16), jnp.asarray(level), norm_w.reshape(1, GLA_DV))


def _t5_bucket_np(dist):
    n = np.maximum(dist, 0)
    large = REL_MAX_EXACT + (np.log(np.maximum(n, 1).astype(np.float32) / REL_MAX_EXACT)
                             / math.log(REL_MAX_DIST / REL_MAX_EXACT)
                             * (REL_BUCKETS - REL_MAX_EXACT)).astype(np.int32)
    large = np.minimum(large, REL_BUCKETS - 1)
    return np.where(n < REL_MAX_EXACT, n, large).astype(np.int32)


def _bias_kernel(rb_ref, bk_ref, o_ref):
    h = pl.program_id(0)
    for which in range(2):
        bk = bk_ref[which]
        acc = jnp.full(bk.shape, NEG_INF, F32)
        for bucket in range(REL_BUCKETS):
            acc = jnp.where(bk == bucket, rb_ref[bucket, h], acc)
        o_ref[0, which] = acc


def _bias_tables(rel_bias):
    blk = MOBA_BLOCK
    j = np.arange(blk)[:, None]
    i = np.arange(blk)[None, :]
    own = np.where(i >= j, _t5_bucket_np(i - j), -1)
    prev = _t5_bucket_np(i + blk - j)
    buckets = np.stack([own, prev]).astype(np.int32)
    return pl.pallas_call(
        _bias_kernel,
        out_shape=jax.ShapeDtypeStruct((MOBA_HEADS, 2, blk, blk), F32),
        grid=(MOBA_HEADS,),
        in_specs=[pl.BlockSpec(memory_space=pltpu.SMEM),
                  pl.BlockSpec((2, blk, blk), lambda h: (0, 0, 0))],
        out_specs=pl.BlockSpec((1, 2, blk, blk), lambda h: (h, 0, 0, 0)),
        compiler_params=_params("parallel"),
        name="moba_bias",
    )(rel_bias, jnp.asarray(buckets))


def _moba_kernel(rb_ref, q_ref, k_ref, v_ref, qn_ref, kn_ref, bias_ref, o_ref, kh_ref, vt_ref, kmh_ref, kml_ref,
                 sel_ref):
    blk, dh, nh = MOBA_BLOCK, MOBA_HEAD_DIM, MOBA_HEADS
    nb = k_ref.shape[1] // blk
    qb = pl.program_id(1)

    @pl.when(qb == 0)
    def _():
        for h in range(nh):
            cols = slice(h * dh, (h + 1) * dh)
            means = []
            for n in range(nb):
                rows = slice(n * blk, (n + 1) * blk)
                kh = _rms(k_ref[0, rows, cols].astype(F32), kn_ref[...])
                kh_ref[h, n] = kh.astype(BF16)
                vt_ref[h, n] = v_ref[0, rows, cols].astype(F32).T.astype(BF16)
                means.append(jnp.mean(kh, axis=0, keepdims=True))
            km_hi, km_lo = _split_hi_lo(jnp.concatenate(means, axis=0))
            kmh_ref[h] = km_hi
            kml_ref[h] = km_lo

    n_iota = lax.broadcasted_iota(jnp.int32, (nb, blk), 0)
    heads = range(nh)
    qh = [_rms(q_ref[0, :, h * dh:(h + 1) * dh].astype(F32), qn_ref[...]) for h in heads]
    q_scs = [(qh[h] * (dh ** -0.5)).astype(BF16) for h in heads]

    s_own = [_dot_nt(kh_ref[h, qb], q_scs[h]) + bias_ref[h, 0] for h in heads]
    m0 = [jnp.max(s_own[h], axis=0, keepdims=True) for h in heads]
    p0 = [jnp.exp(s_own[h] - m0[h]) for h in heads]
    carry = [(m0[h], jnp.sum(p0[h], axis=0, keepdims=True), _dot(vt_ref[h, qb], p0[h].astype(BF16))) for h in heads]

    for h in heads:
        q_hi, q_lo = _split_hi_lo(qh[h])
        sc = _dot_nt(kmh_ref[h], q_hi) + _dot_nt(kmh_ref[h], q_lo) + _dot_nt(kml_ref[h], q_hi)
        rank = jnp.zeros((nb, blk), jnp.int32)
        for m in range(nb):
            sm = sc[m:m + 1, :]
            beats = ((sm > sc) | ((sm == sc) & (m < n_iota))) & (m < qb)
            rank = rank + beats.astype(jnp.int32)
        sel_ref[h] = jnp.where((n_iota < qb) & (rank < MOBA_TOPK), 0.0, NEG_INF)

    def attend(n, bias_of, carry):
        heads = range(nh)
        s = [_dot_nt(kh_ref[h, n], q_scs[h]) + bias_of(h) for h in heads]
        m_new = [jnp.maximum(carry[h][0], jnp.max(s[h], axis=0, keepdims=True)) for h in heads]
        alpha = [jnp.exp(carry[h][0] - m_new[h]) for h in heads]
        p = [jnp.exp(s[h] - m_new[h]) for h in heads]
        l_new = [alpha[h] * carry[h][1] + jnp.sum(p[h], axis=0, keepdims=True) for h in heads]
        pv = [_dot(vt_ref[h, n], p[h].astype(BF16)) for h in heads]
        return [(m_new[h], l_new[h], alpha[h] * carry[h][2] + pv[h]) for h in heads]

    prev = jnp.maximum(qb - 1, 0)
    carry = lax.cond(qb > 0,
                     lambda c: attend(prev, lambda h: bias_ref[h, 1] + sel_ref[h, pl.ds(prev, 1), :], c),
                     lambda c: c, carry)
    carry = lax.fori_loop(
        0, prev, lambda n, c: attend(n, lambda h: sel_ref[h, pl.ds(n, 1), :] + rb_ref[REL_BUCKETS - 1, h], c), carry)
    o_ref[0] = jnp.concatenate([(acc / l).T for (_, l, acc) in carry], axis=1).astype(BF16)


def _moba(proj, qn_w, kn_w, rel_bias, bias_tab):
    b, s, _ = proj.shape
    blk, dh, nh = MOBA_BLOCK, MOBA_HEAD_DIM, MOBA_HEADS
    c = nh * dh
    nb = s // blk
    return pl.pallas_call(
        _moba_kernel,
        out_shape=jax.ShapeDtypeStruct((b, s, c), BF16),
        grid=(b, nb),
        in_specs=[pl.BlockSpec(memory_space=pltpu.SMEM),
                  pl.BlockSpec((1, blk, c), lambda i, j: (i, j, _OFF["mq"] // c)),
                  pl.BlockSpec((1, s, c), lambda i, j: (i, 0, _OFF["mk"] // c)),
                  pl.BlockSpec((1, s, c), lambda i, j: (i, 0, _OFF["mv"] // c)),
                  pl.BlockSpec((1, dh), lambda i, j: (0, 0)),
                  pl.BlockSpec((1, dh), lambda i, j: (0, 0)),
                  pl.BlockSpec((nh, 2, blk, blk), lambda i, j: (0, 0, 0, 0))],
        out_specs=pl.BlockSpec((1, blk, c), lambda i, j: (i, j, 0)),
        scratch_shapes=[pltpu.VMEM((nh, nb, blk, dh), BF16), pltpu.VMEM((nh, nb, dh, blk), BF16),
                        pltpu.VMEM((nh, nb, dh), BF16), pltpu.VMEM((nh, nb, dh), BF16),
                        pltpu.VMEM((nh, nb, blk), F32)],
        compiler_params=_params("parallel", "arbitrary"),
        name="moba",
    )(rel_bias, proj, proj, proj, qn_w.reshape(1, dh), kn_w.reshape(1, dh), bias_tab)


def _pool_branch(u_ref, halo_ref, pw_ref, ps_ref, start):
    tm = u_ref.shape[0]
    cur = u_ref[...].astype(F32)
    halo = jnp.where(start > 0, halo_ref[...].astype(F32), 0.0)
    ext = jnp.concatenate([halo, cur], axis=0)
    pos = start + lax.broadcasted_iota(jnp.int32, (tm, 1), 0)
    outs = []
    for g, w in enumerate(POOL_WINDOWS):
        cols = slice(g * POOL_GROUP_DIM, (g + 1) * POOL_GROUP_DIM)
        a = ext[:, cols]
        shift = 1
        while shift < w:
            a = a + pltpu.roll(a, shift, 0)
            shift *= 2
        cnt = jnp.minimum(pos + 1, w).astype(F32)
        p = a[POOL_HALO:, :] / cnt - cur[:, cols]
        outs.append(_dot(p.astype(BF16), pw_ref[g]))
    return jnp.concatenate(outs, axis=1) * ps_ref[...]


def _mix_kernel(seq, x_ref, ya_ref, yc_ref, u_ref, halo_ref, ga0_ref, ga1_ref, gb0_ref, gb1_ref, gc0_ref, gc1_ref,
                pw_ref, ps_ref, wa_ref, wb_ref, wc_ref, wo_ref, n2_ref, wg_ref, wu_ref, wd_ref, o_ref):
    tm = x_ref.shape[0]
    start = (pl.program_id(0) * tm) % seq
    yb = _pool_branch(u_ref, halo_ref, pw_ref, ps_ref, start)
    gate = lambda lo_ref, hi_ref: _sigmoid(jnp.concatenate([lo_ref[...], hi_ref[...]], axis=1).astype(F32))
    merged = (gate(ga0_ref, ga1_ref) * _dot(ya_ref[...], wa_ref[...])
              + gate(gb0_ref, gb1_ref) * _dot(yb.astype(BF16), wb_ref[...])
              + gate(gc0_ref, gc1_ref) * _dot(yc_ref[...], wc_ref[...]))
    x1 = x_ref[...] + _dot(merged.astype(BF16), wo_ref[...])
    h = _rms(x1, n2_ref[...]).astype(BF16)
    acc = x1
    for c in range(FFN_NCHUNK):
        cols = slice(c * FFN_CHUNK, (c + 1) * FFN_CHUNK)
        g = _dot(h, wg_ref[:, cols])
        u = _dot(h, wu_ref[:, cols])
        act = (g * _sigmoid(g) * u).astype(BF16)
        acc = acc + _dot(act, wd_ref[cols, :])
    o_ref[...] = acc


def _mix(x2d, ya, yc, proj2d, seq, layer, pool_w, pool_scale, wa, wb, wc, wo, norm2_w, wg, wu, wd, tm):
    m = x2d.shape[0]
    c = BRANCH_WIDTH
    assert seq % tm == 0 and tm % POOL_HALO == 0 and POOL_HALO == BF16_SUBLANES
    row = lambda w, col: pl.BlockSpec((tm, w), lambda i: (i, col))
    resident = lambda *shape: pl.BlockSpec((None,) + shape, lambda i: (layer,) + (0,) * len(shape),
                                           pipeline_mode=pl.Buffered(1))
    halo_blocks = tm // POOL_HALO
    gates = [row(c, _OFF[g] // c + half) for g in ("ga", "gb", "gc") for half in range(D_MODEL // c)]
    return pl.pallas_call(
        functools.partial(_mix_kernel, seq),
        out_shape=jax.ShapeDtypeStruct((m, D_MODEL), F32),
        grid=(m // tm,),
        in_specs=[row(D_MODEL, 0), row(c, 0), row(c, 0), row(c, _OFF["pu"] // c),
                  pl.BlockSpec((POOL_HALO, c), lambda i: (jnp.maximum(i * halo_blocks - 1, 0), _OFF["pu"] // c)),
                  *gates,
                  resident(POOL_GROUPS, POOL_GROUP_DIM, POOL_GROUP_DIM), resident(1, c),
                  resident(c, D_MODEL), resident(c, D_MODEL), resident(c, D_MODEL), resident(D_MODEL, D_MODEL),
                  resident(1, D_MODEL), resident(D_MODEL, D_FF), resident(D_MODEL, D_FF), resident(D_FF, D_MODEL)],
        out_specs=row(D_MODEL, 0),
        compiler_params=_params("parallel"),
        name="mix",
    )(x2d, ya, yc, proj2d, proj2d, *([proj2d] * len(gates)), pool_w, pool_scale,
      wa, wb, wc, wo, norm2_w, wg, wu, wd)


def _relayout_w_in(w):
    glr_end = GLR_SRC + GLA_LOW_RANK
    pad = jnp.zeros(w.shape[:-1] + (LANES - GLA_LOW_RANK,), BF16)
    return jnp.concatenate([w[..., :GLR_SRC].astype(BF16), w[..., glr_end:].astype(BF16),
                            w[..., GLR_SRC:glr_end].astype(BF16), pad], axis=-1)


def _row_tile(m, want):
    t = min(m, want)
    assert m % t == 0
    return t


def kernel(x, norm1_w, w_in, gla_wg2, gla_bg, gla_norm_w, pool_w, pool_scale, moba_qn_w, moba_kn_w,
           rel_bias, w_up_a, w_up_b, w_up_c, w_out, norm2_w, ffn_w_gate, ffn_w_up, ffn_w_down):
    b, s, d = x.shape
    assert d == D_MODEL and s % MOBA_BLOCK == 0 and s % GLA_T == 0
    m = b * s
    bf = lambda a: a.astype(BF16)
    bias_tab = _bias_tables(rel_bias)
    w_p = _relayout_w_in(w_in)
    norm1 = norm1_w.reshape(DEPTH, 1, d)
    mix_weights = (bf(pool_w), pool_scale.reshape(DEPTH, 1, -1), bf(w_up_a), bf(w_up_b), bf(w_up_c), bf(w_out),
                   norm2_w.reshape(DEPTH, 1, d), bf(ffn_w_gate), bf(ffn_w_up), bf(ffn_w_down))
    x2d = x.reshape(m, d)
    for l in range(DEPTH):
        proj2d = _inproj(x2d, norm1, w_p, l, _row_tile(m, 512))
        proj = proj2d.reshape(b, s, PROJ_COLS)
        ya = _gla(proj, gla_wg2[l], gla_bg[l], gla_norm_w[l])
        yc = _moba(proj, moba_qn_w[l], moba_kn_w[l], rel_bias, bias_tab)
        x2d = _mix(x2d, ya.reshape(m, -1), yc.reshape(m, -1), proj2d, s, l, *mix_weights, _row_tile(s, 512))
    return x2d.reshape(b, s, d)
```

```python
import functools
import math

import numpy as np
import jax
import jax.numpy as jnp
from jax import lax
from jax.experimental import pallas as pl
from jax.experimental.pallas import tpu as pltpu

D_MODEL = 1024
DEPTH = 2
BRANCH_WIDTH = D_MODEL // 2
GLA_HEADS = 4
GLA_DV = BRANCH_WIDTH // GLA_HEADS
GLA_DK = GLA_DV // 2
GLA_LOW_RANK = 16
GLA_GATE_NORMALIZER = 16.0
POOL_WINDOWS = (2, 4, 8, 16)
POOL_GROUPS = 4
POOL_GROUP_DIM = BRANCH_WIDTH // POOL_GROUPS
MOBA_HEADS = 4
MOBA_HEAD_DIM = BRANCH_WIDTH // MOBA_HEADS
MOBA_BLOCK = 256
MOBA_TOPK = 3
REL_BUCKETS = 32
REL_MAX_EXACT = REL_BUCKETS // 2
REL_MAX_DIST = 128
D_FF = ((8 * D_MODEL + 767) // 768) * 256
NORM_EPS = 1e-6
NEG_INF = -1e30
LOG2E = math.log2(math.e)

LANES = 128
BF16_SUBLANES = 16
VMEM_LIMIT_BYTES = 56 * 1024 * 1024

_SECTIONS = (("gq", 256), ("gk", 256), ("gv", 512), ("glr", GLA_LOW_RANK), ("gr", 512), ("pu", 512),
             ("mq", 512), ("mk", 512), ("mv", 512), ("ga", 1024), ("gb", 1024), ("gc", 1024))
_OFF = {}
_o = 0
for _n, _w in _SECTIONS:
    if _n != "glr":
        assert _o % min(_w, 512) == 0
        _OFF[_n] = _o
        _o += _w
_OFF["glr"] = _o
PROJ_COLS = _o + LANES
GLR_SRC = sum(w for n, w in _SECTIONS[:3])
PROJ_CHUNK = 512

GLA_T = 128
GLA_LEVELS = 7
GLA_BATCH_PER_STEP = 4
MOBA_BATCH_PER_STEP = 2
POOL_HALO = 16
FFN_CHUNK = 256
FFN_NCHUNK = D_FF // FFN_CHUNK

F32 = jnp.float32
BF16 = jnp.bfloat16


def _params(*sem):
    return pltpu.CompilerParams(dimension_semantics=sem, vmem_limit_bytes=VMEM_LIMIT_BYTES)


def _dot(a, b):
    return jnp.dot(a, b, preferred_element_type=F32)


def _dot_nt(a, b):
    return lax.dot_general(a, b, (((1,), (1,)), ((), ())), preferred_element_type=F32)


def _split_hi_lo(x):
    hi = x.astype(BF16)
    lo = (x - hi.astype(F32)).astype(BF16)
    return hi, lo


def _rms(x, w):
    return x * lax.rsqrt(jnp.mean(x * x, axis=-1, keepdims=True) + NORM_EPS) * w


def _sigmoid(x):
    return 1.0 / (1.0 + jnp.exp(-x))


def _inproj_kernel(x_ref, nw_ref, w_ref, o_ref, wp_ref):
    @pl.when(pl.program_id(0) == 0)
    def _():
        glr_end = GLR_SRC + GLA_LOW_RANK
        wp_ref[:, :GLR_SRC] = w_ref[:, :GLR_SRC]
        wp_ref[:, GLR_SRC:_OFF["glr"]] = w_ref[:, glr_end:]
        wp_ref[:, _OFF["glr"]:] = jnp.concatenate(
            [w_ref[:, GLR_SRC:glr_end], jnp.zeros((D_MODEL, LANES - GLA_LOW_RANK), BF16)], axis=1)

    h = _rms(x_ref[...], nw_ref[...]).astype(BF16)
    for lo in range(0, PROJ_COLS, PROJ_CHUNK):
        cols = slice(lo, min(lo + PROJ_CHUNK, PROJ_COLS))
        o_ref[:, cols] = _dot(h, wp_ref[:, cols]).astype(BF16)


def _inproj(x2d, norm_w, w_bf, layer, tm):
    m = x2d.shape[0]
    in_cols = w_bf.shape[-1]
    return pl.pallas_call(
        _inproj_kernel,
        out_shape=jax.ShapeDtypeStruct((m, PROJ_COLS), BF16),
        grid=(m // tm,),
        in_specs=[pl.BlockSpec((tm, D_MODEL), lambda i: (i, 0)),
                  pl.BlockSpec((None, 1, D_MODEL), lambda i: (layer, 0, 0)),
                  pl.BlockSpec((None, D_MODEL, in_cols), lambda i: (layer, 0, 0), pipeline_mode=pl.Buffered(1))],
        out_specs=pl.BlockSpec((tm, PROJ_COLS), lambda i: (i, 0)),
        scratch_shapes=[pltpu.VMEM((D_MODEL, PROJ_COLS), BF16)],
        compiler_params=_params("arbitrary"),
        name="inproj",
    )(x2d, norm_w, w_bf)


def _gla_constants():
    t = np.arange(GLA_T)
    u = np.arange(GLA_T)
    a_rows = []
    level = np.full((GLA_T, GLA_T), -1, np.int32)
    for lev in range(GLA_LEVELS):
        s = (GLA_T // 2) >> lev
        g = 2 * s
        first = (t % g) < s
        ref = (t // g) * g + s - 1
        a = np.where(first[:, None],
                     (u[None, :] > t[:, None]) & (u[None, :] <= ref[:, None]),
                     (u[None, :] > ref[:, None]) & (u[None, :] <= t[:, None]))
        a_rows.append(a)
        same = (t[:, None] // g) == (t[None, :] // g)
        level[same & (~first)[:, None] & first[None, :]] = lev
    a_rows.append(u[None, :] <= t[:, None])
    level[np.eye(GLA_T, dtype=bool)] = GLA_LEVELS
    a_all = np.concatenate(a_rows, axis=0).astype(np.float32)
    akt = np.concatenate([(u[:, None] > t[None, :]), np.ones((GLA_T, GLA_T), bool)], axis=1).astype(np.float32)
    return np.concatenate([a_all, a_all], axis=1), np.concatenate([akt, akt], axis=0), level


def _log_sigmoid(z):
    return jnp.minimum(z, 0.0) - jnp.log1p(jnp.exp(-jnp.abs(z)))


def _gla_kernel(q_ref, k_ref, v_ref, r_ref, glr_ref, wg2_ref, bg_ref, a_ref, akt_ref, lvl_ref, nw_ref, o_ref,
                s_ref, sbd_ref):
    nh, dk, dv, t = GLA_HEADS, GLA_DK, GLA_DV, GLA_T

    @pl.when(pl.program_id(1) == 0)
    def _():
        s_ref[...] = jnp.zeros_like(s_ref)
        sbd_ref[...] = jnp.zeros_like(sbd_ref)

    rows = range(q_ref.shape[0])
    each = lambda f: [f(bi) for bi in rows]
    q8 = each(lambda bi: q_ref[bi] * (dk ** -0.5))
    k = each(lambda bi: k_ref[bi])
    v = each(lambda bi: v_ref[bi])
    s_old = each(lambda bi: [s_ref[bi, h] for h in range(nh)])

    inv = 1.0 / GLA_GATE_NORMALIZER
    glog = each(lambda bi: _log_sigmoid(_dot(glr_ref[bi], wg2_ref[...]) + bg_ref[...]) * inv)
    glog_t = each(lambda bi: glog[bi].T)

    dec_all = each(lambda bi: jnp.exp(_dot(a_ref[...], jnp.concatenate(_split_hi_lo(glog[bi]), axis=0))).astype(BF16))
    et_all = each(lambda bi: _dot(jnp.concatenate(_split_hi_lo(glog_t[bi]), axis=1), akt_ref[...]))

    qe = each(lambda bi: q8[bi] * dec_all[bi][GLA_LEVELS * t:(GLA_LEVELS + 1) * t])
    o_inter = each(lambda bi: _dot(qe[bi], sbd_ref[bi]))

    lane_head = lax.broadcasted_iota(jnp.int32, (t, nh * dk), 1) // dk
    head_masks = [lane_head == h for h in range(nh)]
    lvl = lvl_ref[...]
    attn = each(lambda bi: [jnp.zeros((t, t), F32) for _ in range(nh)])
    for lev in range(GLA_LEVELS + 1):
        m = lvl == lev
        for bi in rows:
            if lev < GLA_LEVELS:
                e = dec_all[bi][lev * t:(lev + 1) * t]
                ql = q8[bi] * e
                kl = k[bi] * e
            else:
                ql = q8[bi]
                kl = k[bi]
            kbd = jnp.concatenate([jnp.where(head_masks[h], kl, jnp.zeros_like(kl)) for h in range(nh)], axis=0)
            sc = _dot_nt(ql, kbd)
            for h in range(nh):
                attn[bi][h] = jnp.where(m, sc[:, h * t:(h + 1) * t], attn[bi][h])

    nw = nw_ref[...]
    for bi in rows:
        outs = []
        for h in range(nh):
            cols = slice(h * dv, (h + 1) * dv)
            outs.append(_rms(_dot(attn[bi][h].astype(BF16), v[bi][:, cols]) + o_inter[bi][:, cols], nw))
        r = r_ref[bi].astype(F32)
        o_ref[bi] = (jnp.concatenate(outs, axis=1) * (r * _sigmoid(r))).astype(BF16)

    k_tail_t = each(lambda bi: (k[bi].astype(F32).T * jnp.exp(et_all[bi][:, :t])).astype(BF16))
    dec_t = each(lambda bi: jnp.exp(et_all[bi][:, t:]))
    for bi in rows:
        for h in range(nh):
            ch = slice(h * dk, (h + 1) * dk)
            s_new = dec_t[bi][ch, :] * s_old[bi][h] + _dot(k_tail_t[bi][ch, :], v[bi][:, h * dv:(h + 1) * dv])
            s_ref[bi, h] = s_new
            sbd_ref[bi, ch, h * dv:(h + 1) * dv] = s_new.astype(BF16)


def _gla(proj, wg2, bg, norm_w):
    b, s, _ = proj.shape
    t = GLA_T
    hk = GLA_HEADS * GLA_DK
    hv = GLA_HEADS * GLA_DV
    a_all, akt, level = _gla_constants()
    wg2_p = jnp.zeros((LANES, hk), F32).at[:GLA_LOW_RANK].set(wg2).astype(BF16)
    const = lambda *shape: pl.BlockSpec(shape, lambda i, j: (0,) * len(shape))
    nb = GLA_BATCH_PER_STEP if b % GLA_BATCH_PER_STEP == 0 else 1
    return pl.pallas_call(
        _gla_kernel,
        out_shape=jax.ShapeDtypeStruct((b, s, hv), BF16),
        grid=(b // nb, s // t),
        in_specs=[pl.BlockSpec((nb, t, hk), lambda i, j: (i, j, _OFF["gq"] // hk)),
                  pl.BlockSpec((nb, t, hk), lambda i, j: (i, j, _OFF["gk"] // hk)),
                  pl.BlockSpec((nb, t, hv), lambda i, j: (i, j, _OFF["gv"] // hv)),
                  pl.BlockSpec((nb, t, hv), lambda i, j: (i, j, _OFF["gr"] // hv)),
                  pl.BlockSpec((nb, t, LANES), lambda i, j: (i, j, _OFF["glr"] // LANES)),
                  const(LANES, hk), const(1, hk),
                  const((GLA_LEVELS + 1) * t, 2 * t), const(2 * t, 2 * t), const(t, t),
                  const(1, GLA_DV)],
        out_specs=pl.BlockSpec((nb, t, hv), lambda i, j: (i, j, 0)),
        scratch_shapes=[pltpu.VMEM((nb, GLA_HEADS, GLA_DK, GLA_DV), F32), pltpu.VMEM((nb, hk, hv), BF16)],
        compiler_params=_params("parallel", "arbitrary"),
        name="gla",
    )(proj, proj, proj, proj, proj, wg2_p, bg.reshape(1, hk),
      jnp.asarray(a_all, BF16), jnp.asarray(akt, BF16), jnp.asarray(level), norm_w.reshape(1, GLA_DV))


def _t5_bucket_np(dist):
    n = np.maximum(dist, 0)
    large = REL_MAX_EXACT + (np.log(np.maximum(n, 1).astype(np.float32) / REL_MAX_EXACT)
                             / math.log(REL_MAX_DIST / REL_MAX_EXACT)
                             * (REL_BUCKETS - REL_MAX_EXACT)).astype(np.int32)
    large = np.minimum(large, REL_BUCKETS - 1)
    return np.where(n < REL_MAX_EXACT, n, large).astype(np.int32)


def _bias_kernel(rb_ref, bk_ref, o_ref):
    h = pl.program_id(0)
    for which in range(2):
        bk = bk_ref[which]
        acc = jnp.full(bk.shape, NEG_INF, F32)
        for bucket in range(REL_BUCKETS):
            acc = jnp.where(bk == bucket, rb_ref[bucket, h] * LOG2E, acc)
        o_ref[0, which] = acc


def _bias_tables(rel_bias):
    blk = MOBA_BLOCK
    j = np.arange(blk)[:, None]
    i = np.arange(blk)[None, :]
    own = np.where(i >= j, _t5_bucket_np(i - j), -1)
    prev = _t5_bucket_np(i + blk - j)
    buckets = np.stack([own, prev]).astype(np.int32)
    return pl.pallas_call(
        _bias_kernel,
        out_shape=jax.ShapeDtypeStruct((MOBA_HEADS, 2, blk, blk), F32),
        grid=(MOBA_HEADS,),
        in_specs=[pl.BlockSpec(memory_space=pltpu.SMEM),
                  pl.BlockSpec((2, blk, blk), lambda h: (0, 0, 0))],
        out_specs=pl.BlockSpec((1, 2, blk, blk), lambda h: (h, 0, 0, 0)),
        compiler_params=_params("parallel"),
        name="moba_bias",
    )(rel_bias, jnp.asarray(buckets))


def _moba_kernel(rb_ref, q_ref, k_ref, v_ref, qn_ref, kn_ref, bias_ref, o_ref, kh_ref, vt_ref, kmh_ref, kml_ref,
                 sel_ref):
    blk, dh, nh = MOBA_BLOCK, MOBA_HEAD_DIM, MOBA_HEADS
    nb = k_ref.shape[1] // blk
    qb = pl.program_id(1)
    chains = [(bi, h) for bi in range(q_ref.shape[0]) for h in range(nh)]
    each = lambda f: [f(c, bi, h) for c, (bi, h) in enumerate(chains)]

    @pl.when(qb == 0)
    def _():
        for c, (bi, h) in enumerate(chains):
            cols = slice(h * dh, (h + 1) * dh)
            means = []
            for n in range(nb):
                rows = slice(n * blk, (n + 1) * blk)
                kh = _rms(k_ref[bi, rows, cols].astype(F32), kn_ref[...])
                kh_ref[c, n] = kh.astype(BF16)
                vt_ref[c, n] = v_ref[bi, rows, cols].astype(F32).T.astype(BF16)
                means.append(jnp.mean(kh, axis=0, keepdims=True))
            km_hi, km_lo = _split_hi_lo(jnp.concatenate(means, axis=0))
            kmh_ref[c] = km_hi
            kml_ref[c] = km_lo

    n_iota = lax.broadcasted_iota(jnp.int32, (nb, blk), 0)
    qh = each(lambda c, bi, h: _rms(q_ref[bi, :, h * dh:(h + 1) * dh].astype(F32), qn_ref[...]))
    q_sc = each(lambda c, bi, h: (qh[c] * (dh ** -0.5 * LOG2E)).astype(BF16))

    s_own = each(lambda c, bi, h: _dot_nt(kh_ref[c, qb], q_sc[c]) + bias_ref[h, 0])
    m0 = each(lambda c, bi, h: jnp.max(s_own[c], axis=0, keepdims=True))
    p0 = each(lambda c, bi, h: jnp.exp2(s_own[c] - m0[c]))
    carry = each(lambda c, bi, h: (m0[c], jnp.sum(p0[c], axis=0, keepdims=True),
                                   _dot(vt_ref[c, qb], p0[c].astype(BF16))))

    for c in range(len(chains)):
        q_hi, q_lo = _split_hi_lo(qh[c])
        sc = _dot_nt(kmh_ref[c], q_hi) + _dot_nt(kmh_ref[c], q_lo) + _dot_nt(kml_ref[c], q_hi)
        rank = jnp.zeros((nb, blk), jnp.int32)
        for m in range(nb):
            sm = sc[m:m + 1, :]
            beats = ((sm > sc) | ((sm == sc) & (m < n_iota))) & (m < qb)
            rank = rank + beats.astype(jnp.int32)
        sel_ref[c] = jnp.where((n_iota < qb) & (rank < MOBA_TOPK), 0.0, NEG_INF)

    def attend(n, bias_of, carry):
        s = each(lambda c, bi, h: _dot_nt(kh_ref[c, n], q_sc[c]) + bias_of(c, h))
        m_new = each(lambda c, bi, h: jnp.maximum(carry[c][0], jnp.max(s[c], axis=0, keepdims=True)))
        alpha = each(lambda c, bi, h: jnp.exp2(carry[c][0] - m_new[c]))
        p = each(lambda c, bi, h: jnp.exp2(s[c] - m_new[c]))
        l_new = each(lambda c, bi, h: alpha[c] * carry[c][1] + jnp.sum(p[c], axis=0, keepdims=True))
        pv = each(lambda c, bi, h: _dot(vt_ref[c, n], p[c].astype(BF16)))
        return each(lambda c, bi, h: (m_new[c], l_new[c], alpha[c] * carry[c][2] + pv[c]))

    prev = jnp.maximum(qb - 1, 0)
    carry = lax.cond(qb > 0,
                     lambda cr: attend(prev, lambda c, h: bias_ref[h, 1] + sel_ref[c, pl.ds(prev, 1), :], cr),
                     lambda cr: cr, carry)
    carry = lax.fori_loop(
        0, prev,
        lambda n, cr: attend(n, lambda c, h: sel_ref[c, pl.ds(n, 1), :] + rb_ref[REL_BUCKETS - 1, h] * LOG2E, cr), carry)
    outs = each(lambda c, bi, h: (carry[c][2] / carry[c][1]).T)
    for bi in range(q_ref.shape[0]):
        o_ref[bi] = jnp.concatenate(outs[bi * nh:(bi + 1) * nh], axis=1).astype(BF16)


def _moba(proj, qn_w, kn_w, rel_bias, bias_tab):
    b, s, _ = proj.shape
    blk, dh, nh = MOBA_BLOCK, MOBA_HEAD_DIM, MOBA_HEADS
    c = nh * dh
    nb = s // blk
    rows = MOBA_BATCH_PER_STEP if b % MOBA_BATCH_PER_STEP == 0 else 1
    nc = rows * nh
    return pl.pallas_call(
        _moba_kernel,
        out_shape=jax.ShapeDtypeStruct((b, s, c), BF16),
        grid=(b // rows, nb),
        in_specs=[pl.BlockSpec(memory_space=pltpu.SMEM),
                  pl.BlockSpec((rows, blk, c), lambda i, j: (i, j, _OFF["mq"] // c)),
                  pl.BlockSpec((rows, s, c), lambda i, j: (i, 0, _OFF["mk"] // c)),
                  pl.BlockSpec((rows, s, c), lambda i, j: (i, 0, _OFF["mv"] // c)),
                  pl.BlockSpec((1, dh), lambda i, j: (0, 0)),
                  pl.BlockSpec((1, dh), lambda i, j: (0, 0)),
                  pl.BlockSpec((nh, 2, blk, blk), lambda i, j: (0, 0, 0, 0))],
        out_specs=pl.BlockSpec((rows, blk, c), lambda i, j: (i, j, 0)),
        scratch_shapes=[pltpu.VMEM((nc, nb, blk, dh), BF16), pltpu.VMEM((nc, nb, dh, blk), BF16),
                        pltpu.VMEM((nc, nb, dh), BF16), pltpu.VMEM((nc, nb, dh), BF16),
                        pltpu.VMEM((nc, nb, blk), F32)],
        compiler_params=_params("parallel", "arbitrary"),
        name="moba",
    )(rel_bias, proj, proj, proj, qn_w.reshape(1, dh), kn_w.reshape(1, dh), bias_tab)


def _pool_branch(u_ref, halo_ref, pw_ref, ps_ref, start):
    tm = u_ref.shape[0]
    cur = u_ref[...].astype(F32)
    halo = jnp.where(start > 0, halo_ref[...].astype(F32), 0.0)
    ext = jnp.concatenate([halo, cur], axis=0)
    pos = start + lax.broadcasted_iota(jnp.int32, (tm, 1), 0)
    outs = []
    for g, w in enumerate(POOL_WINDOWS):
        cols = slice(g * POOL_GROUP_DIM, (g + 1) * POOL_GROUP_DIM)
        a = ext[:, cols]
        shift = 1
        while shift < w:
            a = a + pltpu.roll(a, shift, 0)
            shift *= 2
        cnt = jnp.minimum(pos + 1, w).astype(F32)
        p = a[POOL_HALO:, :] / cnt - cur[:, cols]
        outs.append(_dot(p.astype(BF16), pw_ref[g]))
    return jnp.concatenate(outs, axis=1) * ps_ref[...]


def _mix_kernel(seq, x_ref, ya_ref, yc_ref, u_ref, halo_ref, ga0_ref, ga1_ref, gb0_ref, gb1_ref, gc0_ref, gc1_ref,
                pw_ref, ps_ref, wa_ref, wb_ref, wc_ref, wo_ref, n2_ref, wg_ref, wu_ref, wd_ref, o_ref):
    tm = x_ref.shape[0]
    start = (pl.program_id(0) * tm) % seq
    yb = _pool_branch(u_ref, halo_ref, pw_ref, ps_ref, start)
    gate = lambda lo_ref, hi_ref: _sigmoid(jnp.concatenate([lo_ref[...], hi_ref[...]], axis=1).astype(F32))
    merged = (gate(ga0_ref, ga1_ref) * _dot(ya_ref[...], wa_ref[...])
              + gate(gb0_ref, gb1_ref) * _dot(yb.astype(BF16), wb_ref[...])
              + gate(gc0_ref, gc1_ref) * _dot(yc_ref[...], wc_ref[...]))
    x1 = x_ref[...] + _dot(merged.astype(BF16), wo_ref[...])
    h = _rms(x1, n2_ref[...]).astype(BF16)
    acc = x1
    for c in range(FFN_NCHUNK):
        cols = slice(c * FFN_CHUNK, (c + 1) * FFN_CHUNK)
        g = _dot(h, wg_ref[:, cols])
        u = _dot(h, wu_ref[:, cols])
        act = (g * _sigmoid(g) * u).astype(BF16)
        acc = acc + _dot(act, wd_ref[cols, :])
    o_ref[...] = acc


def _mix(x2d, ya, yc, proj2d, seq, layer, pool_w, pool_scale, wa, wb, wc, wo, norm2_w, wg, wu, wd, tm):
    m = x2d.shape[0]
    c = BRANCH_WIDTH
    assert seq % tm == 0 and tm % POOL_HALO == 0 and POOL_HALO == BF16_SUBLANES
    row = lambda w, col: pl.BlockSpec((tm, w), lambda i: (i, col))
    resident = lambda *shape: pl.BlockSpec((None,) + shape, lambda i: (layer,) + (0,) * len(shape),
                                           pipeline_mode=pl.Buffered(1))
    halo_blocks = tm // POOL_HALO
    gates = [row(c, _OFF[g] // c + half) for g in ("ga", "gb", "gc") for half in range(D_MODEL // c)]
    return pl.pallas_call(
        functools.partial(_mix_kernel, seq),
        out_shape=jax.ShapeDtypeStruct((m, D_MODEL), F32),
        grid=(m // tm,),
        in_specs=[row(D_MODEL, 0), row(c, 0), row(c, 0), row(c, _OFF["pu"] // c),
                  pl.BlockSpec((POOL_HALO, c), lambda i: (jnp.maximum(i * halo_blocks - 1, 0), _OFF["pu"] // c)),
                  *gates,
                  resident(POOL_GROUPS, POOL_GROUP_DIM, POOL_GROUP_DIM), resident(1, c),
                  resident(c, D_MODEL), resident(c, D_MODEL), resident(c, D_MODEL), resident(D_MODEL, D_MODEL),
                  resident(1, D_MODEL), resident(D_MODEL, D_FF), resident(D_MODEL, D_FF), resident(D_FF, D_MODEL)],
        out_specs=row(D_MODEL, 0),
        compiler_params=_params("parallel"),
        name="mix",
    )(x2d, ya, yc, proj2d, proj2d, *([proj2d] * len(gates)), pool_w, pool_scale,
      wa, wb, wc, wo, norm2_w, wg, wu, wd)


def _row_tile(m, want):
    t = min(m, want)
    assert m % t == 0
    return t


def kernel(x, norm1_w, w_in, gla_wg2, gla_bg, gla_norm_w, pool_w, pool_scale, moba_qn_w, moba_kn_w,
           rel_bias, w_up_a, w_up_b, w_up_c, w_out, norm2_w, ffn_w_gate, ffn_w_up, ffn_w_down):
    b, s, d = x.shape
    assert d == D_MODEL and s % MOBA_BLOCK == 0 and s % GLA_T == 0
    m = b * s
    bf = lambda a: a.astype(BF16)
    bias_tab = _bias_tables(rel_bias)
    w_in_bf = bf(w_in)
    norm1 = norm1_w.reshape(DEPTH, 1, d)
    mix_weights = (bf(pool_w), pool_scale.reshape(DEPTH, 1, -1), bf(w_up_a), bf(w_up_b), bf(w_up_c), bf(w_out),
                   norm2_w.reshape(DEPTH, 1, d), bf(ffn_w_gate), bf(ffn_w_up), bf(ffn_w_down))
    x2d = x.reshape(m, d)
    for l in range(DEPTH):
        proj2d = _inproj(x2d, norm1, w_in_bf, l, _row_tile(m, 512))
        proj = proj2d.reshape(b, s, PROJ_COLS)
        ya = _gla(proj, gla_wg2[l], gla_bg[l], gla_norm_w[l])
        yc = _moba(proj, moba_qn_w[l], moba_kn_w[l], rel_bias, bias_tab)
        x2d = _mix(x2d, ya.reshape(m, -1), yc.reshape(m, -1), proj2d, s, l, *mix_weights, _row_tile(s, 512))
    return x2d.reshape(b, s, d)
```

```python
import functools
import math

import numpy as np
import jax
import jax.numpy as jnp
from jax import lax
from jax.experimental import pallas as pl
from jax.experimental.pallas import tpu as pltpu

D_MODEL = 1024
DEPTH = 2
BRANCH_WIDTH = D_MODEL // 2
GLA_HEADS = 4
GLA_DV = BRANCH_WIDTH // GLA_HEADS
GLA_DK = GLA_DV // 2
GLA_LOW_RANK = 16
GLA_GATE_NORMALIZER = 16.0
POOL_WINDOWS = (2, 4, 8, 16)
POOL_GROUPS = 4
POOL_GROUP_DIM = BRANCH_WIDTH // POOL_GROUPS
MOBA_HEADS = 4
MOBA_HEAD_DIM = BRANCH_WIDTH // MOBA_HEADS
MOBA_BLOCK = 256
MOBA_TOPK = 3
REL_BUCKETS = 32
REL_MAX_EXACT = REL_BUCKETS // 2
REL_MAX_DIST = 128
D_FF = ((8 * D_MODEL + 767) // 768) * 256
NORM_EPS = 1e-6
NEG_INF = -1e30
LOG2E = math.log2(math.e)

LANES = 128
BF16_SUBLANES = 16
VMEM_LIMIT_BYTES = 56 * 1024 * 1024

_SECTIONS = (("gq", 256), ("gk", 256), ("gv", 512), ("glr", GLA_LOW_RANK), ("gr", 512), ("pu", 512),
             ("mq", 512), ("mk", 512), ("mv", 512), ("ga", 1024), ("gb", 1024), ("gc", 1024))
_OFF = {}
_o = 0
for _n, _w in _SECTIONS:
    if _n != "glr":
        assert _o % min(_w, 512) == 0
        _OFF[_n] = _o
        _o += _w
_OFF["glr"] = _o
PROJ_COLS = _o + LANES
GLR_SRC = sum(w for n, w in _SECTIONS[:3])
PROJ_CHUNK = 512

GLA_T = 128
GLA_LEVELS = 7
GLA_BATCH_PER_STEP = 4
MOBA_BATCH_PER_STEP = 2
POOL_HALO = 16
FFN_CHUNK = 256
FFN_NCHUNK = D_FF // FFN_CHUNK

F32 = jnp.float32
BF16 = jnp.bfloat16


def _params(*sem):
    return pltpu.CompilerParams(dimension_semantics=sem, vmem_limit_bytes=VMEM_LIMIT_BYTES)


def _dot(a, b):
    return jnp.dot(a, b, preferred_element_type=F32)


def _dot_nt(a, b):
    return lax.dot_general(a, b, (((1,), (1,)), ((), ())), preferred_element_type=F32)


def _split_hi_lo(x):
    hi = x.astype(BF16)
    lo = (x - hi.astype(F32)).astype(BF16)
    return hi, lo


def _rms(x, w):
    return x * lax.rsqrt(jnp.mean(x * x, axis=-1, keepdims=True) + NORM_EPS) * w


def _sigmoid(x):
    return 1.0 / (1.0 + jnp.exp(-x))


def _head_rms(y, w, scale=1.0):
    dh = MOBA_HEAD_DIM
    return jnp.concatenate([_rms(y[:, h * dh:(h + 1) * dh], w) * scale for h in range(y.shape[1] // dh)], axis=1)


def _inproj_kernel(x_ref, nw_ref, w_ref, qn_ref, kn_ref, o_ref, km_ref, wp_ref):
    @pl.when(pl.program_id(0) == 0)
    def _():
        glr_end = GLR_SRC + GLA_LOW_RANK
        wp_ref[:, :GLR_SRC] = w_ref[:, :GLR_SRC]
        wp_ref[:, GLR_SRC:_OFF["glr"]] = w_ref[:, glr_end:]
        wp_ref[:, _OFF["glr"]:] = jnp.concatenate(
            [w_ref[:, GLR_SRC:glr_end], jnp.zeros((D_MODEL, LANES - GLA_LOW_RANK), BF16)], axis=1)

    h = _rms(x_ref[...], nw_ref[...]).astype(BF16)
    for lo in range(0, PROJ_COLS, PROJ_CHUNK):
        cols = slice(lo, min(lo + PROJ_CHUNK, PROJ_COLS))
        y = _dot(h, wp_ref[:, cols])
        if lo == _OFF["gr"]:
            y = y * _sigmoid(y)
        elif lo == _OFF["mq"]:
            y = _head_rms(y, qn_ref[...], MOBA_HEAD_DIM ** -0.5 * LOG2E)
        elif lo == _OFF["mk"]:
            y = _head_rms(y, kn_ref[...])
            for j in range(y.shape[0] // MOBA_BLOCK):
                km_ref[j] = jnp.mean(y[j * MOBA_BLOCK:(j + 1) * MOBA_BLOCK], axis=0, keepdims=True)
        o_ref[:, cols] = y.astype(BF16)


def _inproj(x2d, norm_w, w_bf, qn_w, kn_w, layer, tm):
    m = x2d.shape[0]
    in_cols = w_bf.shape[-1]
    assert tm % MOBA_BLOCK == 0 and all(_OFF[n] % PROJ_CHUNK == 0 for n in ("gr", "mq", "mk"))
    layer_block = lambda *shape: pl.BlockSpec((None,) + shape, lambda i: (layer,) + (0,) * len(shape))
    return pl.pallas_call(
        _inproj_kernel,
        out_shape=(jax.ShapeDtypeStruct((m, PROJ_COLS), BF16),
                   jax.ShapeDtypeStruct((m // MOBA_BLOCK, 1, BRANCH_WIDTH), F32)),
        grid=(m // tm,),
        in_specs=[pl.BlockSpec((tm, D_MODEL), lambda i: (i, 0)),
                  layer_block(1, D_MODEL),
                  pl.BlockSpec((None, D_MODEL, in_cols), lambda i: (layer, 0, 0), pipeline_mode=pl.Buffered(1)),
                  layer_block(1, MOBA_HEAD_DIM), layer_block(1, MOBA_HEAD_DIM)],
        out_specs=(pl.BlockSpec((tm, PROJ_COLS), lambda i: (i, 0)),
                   pl.BlockSpec((tm // MOBA_BLOCK, 1, BRANCH_WIDTH), lambda i: (i, 0, 0))),
        scratch_shapes=[pltpu.VMEM((D_MODEL, PROJ_COLS), BF16)],
        compiler_params=_params("arbitrary"),
        name="inproj",
    )(x2d, norm_w, w_bf, qn_w, kn_w)


def _gla_constants():
    t = np.arange(GLA_T)
    u = np.arange(GLA_T)
    a_rows = []
    level = np.full((GLA_T, GLA_T), -1, np.int32)
    for lev in range(GLA_LEVELS):
        s = (GLA_T // 2) >> lev
        g = 2 * s
        first = (t % g) < s
        ref = (t // g) * g + s - 1
        a = np.where(first[:, None],
                     (u[None, :] > t[:, None]) & (u[None, :] <= ref[:, None]),
                     (u[None, :] > ref[:, None]) & (u[None, :] <= t[:, None]))
        a_rows.append(a)
        same = (t[:, None] // g) == (t[None, :] // g)
        level[same & (~first)[:, None] & first[None, :]] = lev
    a_rows.append(u[None, :] <= t[:, None])
    level[np.eye(GLA_T, dtype=bool)] = GLA_LEVELS
    a_all = np.concatenate(a_rows, axis=0).astype(np.float32)
    akt = np.concatenate([(u[:, None] > t[None, :]), np.ones((GLA_T, GLA_T), bool)], axis=1).astype(np.float32)
    return np.concatenate([a_all, a_all], axis=1), np.concatenate([akt, akt], axis=0), level


def _log_sigmoid(z):
    return jnp.minimum(z, 0.0) - jnp.log1p(jnp.exp(-jnp.abs(z)))


def _gla_kernel(q_ref, k_ref, v_ref, sr_ref, glr_ref, wg2_ref, bg_ref, a_ref, akt_ref, lvl_ref, nw_ref, o_ref,
                s_ref, sbd_ref):
    nh, dk, dv, t = GLA_HEADS, GLA_DK, GLA_DV, GLA_T

    @pl.when(pl.program_id(1) == 0)
    def _():
        s_ref[...] = jnp.zeros_like(s_ref)
        sbd_ref[...] = jnp.zeros_like(sbd_ref)

    rows = range(q_ref.shape[0])
    each = lambda f: [f(bi) for bi in rows]
    q8 = each(lambda bi: q_ref[bi] * (dk ** -0.5))
    k = each(lambda bi: k_ref[bi])
    v = each(lambda bi: v_ref[bi])
    s_old = each(lambda bi: [s_ref[bi, h] for h in range(nh)])

    inv = 1.0 / GLA_GATE_NORMALIZER
    glog = each(lambda bi: _log_sigmoid(_dot(glr_ref[bi], wg2_ref[...]) + bg_ref[...]) * inv)
    glog_t = each(lambda bi: glog[bi].T)

    dec_all = each(lambda bi: jnp.exp(_dot(a_ref[...], jnp.concatenate(_split_hi_lo(glog[bi]), axis=0))).astype(BF16))
    et_all = each(lambda bi: _dot(jnp.concatenate(_split_hi_lo(glog_t[bi]), axis=1), akt_ref[...]))

    qe = each(lambda bi: q8[bi] * dec_all[bi][GLA_LEVELS * t:(GLA_LEVELS + 1) * t])
    o_inter = each(lambda bi: _dot(qe[bi], sbd_ref[bi]))

    lane_head = lax.broadcasted_iota(jnp.int32, (t, nh * dk), 1) // dk
    head_masks = [lane_head == h for h in range(nh)]
    lvl = lvl_ref[...]
    attn = each(lambda bi: [jnp.zeros((t, t), F32) for _ in range(nh)])
    for lev in range(GLA_LEVELS + 1):
        m = lvl == lev
        for bi in rows:
            if lev < GLA_LEVELS:
                e = dec_all[bi][lev * t:(lev + 1) * t]
                ql = q8[bi] * e
                kl = k[bi] * e
            else:
                ql = q8[bi]
                kl = k[bi]
            kbd = jnp.concatenate([jnp.where(head_masks[h], kl, jnp.zeros_like(kl)) for h in range(nh)], axis=0)
            sc = _dot_nt(ql, kbd)
            for h in range(nh):
                attn[bi][h] = jnp.where(m, sc[:, h * t:(h + 1) * t], attn[bi][h])

    nw = nw_ref[...]
    for bi in rows:
        outs = []
        for h in range(nh):
            cols = slice(h * dv, (h + 1) * dv)
            outs.append(_rms(_dot(attn[bi][h].astype(BF16), v[bi][:, cols]) + o_inter[bi][:, cols], nw))
        o_ref[bi] = (jnp.concatenate(outs, axis=1) * sr_ref[bi].astype(F32)).astype(BF16)

    k_tail_t = each(lambda bi: (k[bi].astype(F32).T * jnp.exp(et_all[bi][:, :t])).astype(BF16))
    dec_t = each(lambda bi: jnp.exp(et_all[bi][:, t:]))
    for bi in rows:
        for h in range(nh):
            ch = slice(h * dk, (h + 1) * dk)
            s_new = dec_t[bi][ch, :] * s_old[bi][h] + _dot(k_tail_t[bi][ch, :], v[bi][:, h * dv:(h + 1) * dv])
            s_ref[bi, h] = s_new
            sbd_ref[bi, ch, h * dv:(h + 1) * dv] = s_new.astype(BF16)


def _gla(proj, wg2, bg, norm_w):
    b, s, _ = proj.shape
    t = GLA_T
    hk = GLA_HEADS * GLA_DK
    hv = GLA_HEADS * GLA_DV
    a_all, akt, level = _gla_constants()
    wg2_p = jnp.zeros((LANES, hk), F32).at[:GLA_LOW_RANK].set(wg2).astype(BF16)
    const = lambda *shape: pl.BlockSpec(shape, lambda i, j: (0,) * len(shape))
    nb = GLA_BATCH_PER_STEP if b % GLA_BATCH_PER_STEP == 0 else 1
    return pl.pallas_call(
        _gla_kernel,
        out_shape=jax.ShapeDtypeStruct((b, s, hv), BF16),
        grid=(b // nb, s // t),
        in_specs=[pl.BlockSpec((nb, t, hk), lambda i, j: (i, j, _OFF["gq"] // hk)),
                  pl.BlockSpec((nb, t, hk), lambda i, j: (i, j, _OFF["gk"] // hk)),
                  pl.BlockSpec((nb, t, hv), lambda i, j: (i, j, _OFF["gv"] // hv)),
                  pl.BlockSpec((nb, t, hv), lambda i, j: (i, j, _OFF["gr"] // hv)),
                  pl.BlockSpec((nb, t, LANES), lambda i, j: (i, j, _OFF["glr"] // LANES)),
                  const(LANES, hk), const(1, hk),
                  const((GLA_LEVELS + 1) * t, 2 * t), const(2 * t, 2 * t), const(t, t),
                  const(1, GLA_DV)],
        out_specs=pl.BlockSpec((nb, t, hv), lambda i, j: (i, j, 0)),
        scratch_shapes=[pltpu.VMEM((nb, GLA_HEADS, GLA_DK, GLA_DV), F32), pltpu.VMEM((nb, hk, hv), BF16)],
        compiler_params=_params("parallel", "arbitrary"),
        name="gla",
    )(proj, proj, proj, proj, proj, wg2_p, bg.reshape(1, hk),
      jnp.asarray(a_all, BF16), jnp.asarray(akt, BF16), jnp.asarray(level), norm_w.reshape(1, GLA_DV))


def _t5_bucket_np(dist):
    n = np.maximum(dist, 0)
    large = REL_MAX_EXACT + (np.log(np.maximum(n, 1).astype(np.float32) / REL_MAX_EXACT)
                             / math.log(REL_MAX_DIST / REL_MAX_EXACT)
                             * (REL_BUCKETS - REL_MAX_EXACT)).astype(np.int32)
    large = np.minimum(large, REL_BUCKETS - 1)
    return np.where(n < REL_MAX_EXACT, n, large).astype(np.int32)


def _bias_kernel(rb_ref, bk_ref, o_ref):
    h = pl.program_id(0)
    for which in range(2):
        bk = bk_ref[which]
        acc = jnp.full(bk.shape, NEG_INF, F32)
        for bucket in range(REL_BUCKETS):
            acc = jnp.where(bk == bucket, rb_ref[bucket, h] * LOG2E, acc)
        o_ref[0, which] = acc


def _bias_tables(rel_bias):
    blk = MOBA_BLOCK
    j = np.arange(blk)[:, None]
    i = np.arange(blk)[None, :]
    own = np.where(i >= j, _t5_bucket_np(i - j), -1)
    prev = _t5_bucket_np(i + blk - j)
    buckets = np.stack([own, prev]).astype(np.int32)
    return pl.pallas_call(
        _bias_kernel,
        out_shape=jax.ShapeDtypeStruct((MOBA_HEADS, 2, blk, blk), F32),
        grid=(MOBA_HEADS,),
        in_specs=[pl.BlockSpec(memory_space=pltpu.SMEM),
                  pl.BlockSpec((2, blk, blk), lambda h: (0, 0, 0))],
        out_specs=pl.BlockSpec((1, 2, blk, blk), lambda h: (h, 0, 0, 0)),
        compiler_params=_params("parallel"),
        name="moba_bias",
    )(rel_bias, jnp.asarray(buckets))


def _moba_kernel(rb_ref, q_ref, k_ref, v_ref, km_ref, bias_ref, o_ref, vt_ref, sel_ref):
    blk, dh, nh = MOBA_BLOCK, MOBA_HEAD_DIM, MOBA_HEADS
    nb = k_ref.shape[1] // blk
    qb = pl.program_id(1)
    chains = [(bi, h) for bi in range(q_ref.shape[0]) for h in range(nh)]
    each = lambda f: [f(c, bi, h) for c, (bi, h) in enumerate(chains)]
    head = lambda h: slice(h * dh, (h + 1) * dh)
    keys = lambda bi, h, n: k_ref[bi, pl.ds(pl.multiple_of(n * blk, blk), blk), head(h)]

    @pl.when(qb == 0)
    def _():
        for c, (bi, h) in enumerate(chains):
            for n in range(nb):
                vt_ref[c, n] = v_ref[bi, n * blk:(n + 1) * blk, head(h)].astype(F32).T.astype(BF16)

    n_iota = lax.broadcasted_iota(jnp.int32, (nb, blk), 0)
    q_sc = each(lambda c, bi, h: q_ref[bi, :, head(h)])

    s_own = each(lambda c, bi, h: _dot_nt(keys(bi, h, qb), q_sc[c]) + bias_ref[h, 0])
    m0 = each(lambda c, bi, h: jnp.max(s_own[c], axis=0, keepdims=True))
    p0 = each(lambda c, bi, h: jnp.exp2(s_own[c] - m0[c]))
    carry = each(lambda c, bi, h: (m0[c], jnp.sum(p0[c], axis=0, keepdims=True),
                                   _dot(vt_ref[c, qb], p0[c].astype(BF16))))

    for c, (bi, h) in enumerate(chains):
        km_hi, km_lo = _split_hi_lo(km_ref[bi * nb:(bi + 1) * nb, 0, head(h)])
        sc = _dot_nt(km_hi, q_sc[c]) + _dot_nt(km_lo, q_sc[c])
        rank = jnp.zeros((nb, blk), jnp.int32)
        for m in range(nb):
            sm = sc[m:m + 1, :]
            beats = ((sm > sc) | ((sm == sc) & (m < n_iota))) & (m < qb)
            rank = rank + beats.astype(jnp.int32)
        sel_ref[c] = jnp.where((n_iota < qb) & (rank < MOBA_TOPK), 0.0, NEG_INF)

    def attend(n, bias_of, carry):
        s = each(lambda c, bi, h: _dot_nt(keys(bi, h, n), q_sc[c]) + bias_of(c, h))
        m_new = each(lambda c, bi, h: jnp.maximum(carry[c][0], jnp.max(s[c], axis=0, keepdims=True)))
        alpha = each(lambda c, bi, h: jnp.exp2(carry[c][0] - m_new[c]))
        p = each(lambda c, bi, h: jnp.exp2(s[c] - m_new[c]))
        l_new = each(lambda c, bi, h: alpha[c] * carry[c][1] + jnp.sum(p[c], axis=0, keepdims=True))
        pv = each(lambda c, bi, h: _dot(vt_ref[c, n], p[c].astype(BF16)))
        return each(lambda c, bi, h: (m_new[c], l_new[c], alpha[c] * carry[c][2] + pv[c]))

    prev = jnp.maximum(qb - 1, 0)
    carry = lax.cond(qb > 0,
                     lambda cr: attend(prev, lambda c, h: bias_ref[h, 1] + sel_ref[c, pl.ds(prev, 1), :], cr),
                     lambda cr: cr, carry)
    carry = lax.fori_loop(
        0, prev,
        lambda n, cr: attend(n, lambda c, h: sel_ref[c, pl.ds(n, 1), :] + rb_ref[REL_BUCKETS - 1, h] * LOG2E, cr), carry)
    outs = each(lambda c, bi, h: (carry[c][2] / carry[c][1]).T)
    for bi in range(q_ref.shape[0]):
        o_ref[bi] = jnp.concatenate(outs[bi * nh:(bi + 1) * nh], axis=1).astype(BF16)


def _moba(proj, kmean, rel_bias, bias_tab):
    b, s, _ = proj.shape
    blk, dh, nh = MOBA_BLOCK, MOBA_HEAD_DIM, MOBA_HEADS
    c = nh * dh
    nb = s // blk
    rows = MOBA_BATCH_PER_STEP if b % MOBA_BATCH_PER_STEP == 0 else 1
    nc = rows * nh
    return pl.pallas_call(
        _moba_kernel,
        out_shape=jax.ShapeDtypeStruct((b, s, c), BF16),
        grid=(b // rows, nb),
        in_specs=[pl.BlockSpec(memory_space=pltpu.SMEM),
                  pl.BlockSpec((rows, blk, c), lambda i, j: (i, j, _OFF["mq"] // c)),
                  pl.BlockSpec((rows, s, c), lambda i, j: (i, 0, _OFF["mk"] // c)),
                  pl.BlockSpec((rows, s, c), lambda i, j: (i, 0, _OFF["mv"] // c)),
                  pl.BlockSpec((rows * nb, 1, c), lambda i, j: (i, 0, 0)),
                  pl.BlockSpec((nh, 2, blk, blk), lambda i, j: (0, 0, 0, 0))],
        out_specs=pl.BlockSpec((rows, blk, c), lambda i, j: (i, j, 0)),
        scratch_shapes=[pltpu.VMEM((nc, nb, dh, blk), BF16), pltpu.VMEM((nc, nb, blk), F32)],
        compiler_params=_params("parallel", "arbitrary"),
        name="moba",
    )(rel_bias, proj, proj, proj, kmean, bias_tab)


def _pool_branch(u_ref, halo_ref, pw_ref, ps_ref, start):
    tm = u_ref.shape[0]
    cur = u_ref[...].astype(F32)
    halo = jnp.where(start > 0, halo_ref[...].astype(F32), 0.0)
    ext = jnp.concatenate([halo, cur], axis=0)
    pos = start + lax.broadcasted_iota(jnp.int32, (tm, 1), 0)
    outs = []
    for g, w in enumerate(POOL_WINDOWS):
        cols = slice(g * POOL_GROUP_DIM, (g + 1) * POOL_GROUP_DIM)
        a = ext[:, cols]
        shift = 1
        while shift < w:
            a = a + pltpu.roll(a, shift, 0)
            shift *= 2
        cnt = jnp.minimum(pos + 1, w).astype(F32)
        p = a[POOL_HALO:, :] / cnt - cur[:, cols]
        outs.append(_dot(p.astype(BF16), pw_ref[g]))
    return jnp.concatenate(outs, axis=1) * ps_ref[...]


def _mix_kernel(seq, x_ref, ya_ref, yc_ref, u_ref, halo_ref, ga0_ref, ga1_ref, gb0_ref, gb1_ref, gc0_ref, gc1_ref,
                pw_ref, ps_ref, wa_ref, wb_ref, wc_ref, wo_ref, n2_ref, wg_ref, wu_ref, wd_ref, o_ref):
    tm = x_ref.shape[0]
    start = (pl.program_id(0) * tm) % seq
    yb = _pool_branch(u_ref, halo_ref, pw_ref, ps_ref, start)
    gate = lambda lo_ref, hi_ref: _sigmoid(jnp.concatenate([lo_ref[...], hi_ref[...]], axis=1).astype(F32))
    merged = (gate(ga0_ref, ga1_ref) * _dot(ya_ref[...], wa_ref[...])
              + gate(gb0_ref, gb1_ref) * _dot(yb.astype(BF16), wb_ref[...])
              + gate(gc0_ref, gc1_ref) * _dot(yc_ref[...], wc_ref[...]))
    x1 = x_ref[...] + _dot(merged.astype(BF16), wo_ref[...])
    h = _rms(x1, n2_ref[...]).astype(BF16)
    acc = x1
    for c in range(FFN_NCHUNK):
        cols = slice(c * FFN_CHUNK, (c + 1) * FFN_CHUNK)
        g = _dot(h, wg_ref[:, cols])
        u = _dot(h, wu_ref[:, cols])
        act = (g * _sigmoid(g) * u).astype(BF16)
        acc = acc + _dot(act, wd_ref[cols, :])
    o_ref[...] = acc


def _mix(x2d, ya, yc, proj2d, seq, layer, pool_w, pool_scale, wa, wb, wc, wo, norm2_w, wg, wu, wd, tm):
    m = x2d.shape[0]
    c = BRANCH_WIDTH
    assert seq % tm == 0 and tm % POOL_HALO == 0 and POOL_HALO == BF16_SUBLANES
    row = lambda w, col: pl.BlockSpec((tm, w), lambda i: (i, col))
    resident = lambda *shape: pl.BlockSpec((None,) + shape, lambda i: (layer,) + (0,) * len(shape),
                                           pipeline_mode=pl.Buffered(1))
    halo_blocks = tm // POOL_HALO
    gates = [row(c, _OFF[g] // c + half) for g in ("ga", "gb", "gc") for half in range(D_MODEL // c)]
    return pl.pallas_call(
        functools.partial(_mix_kernel, seq),
        out_shape=jax.ShapeDtypeStruct((m, D_MODEL), F32),
        grid=(m // tm,),
        in_specs=[row(D_MODEL, 0), row(c, 0), row(c, 0), row(c, _OFF["pu"] // c),
                  pl.BlockSpec((POOL_HALO, c), lambda i: (jnp.maximum(i * halo_blocks - 1, 0), _OFF["pu"] // c)),
                  *gates,
                  resident(POOL_GROUPS, POOL_GROUP_DIM, POOL_GROUP_DIM), resident(1, c),
                  resident(c, D_MODEL), resident(c, D_MODEL), resident(c, D_MODEL), resident(D_MODEL, D_MODEL),
                  resident(1, D_MODEL), resident(D_MODEL, D_FF), resident(D_MODEL, D_FF), resident(D_FF, D_MODEL)],
        out_specs=row(D_MODEL, 0),
        compiler_params=_params("parallel"),
        name="mix",
    )(x2d, ya, yc, proj2d, proj2d, *([proj2d] * len(gates)), pool_w, pool_scale,
      wa, wb, wc, wo, norm2_w, wg, wu, wd)


def _row_tile(m, want):
    t = min(m, want)
    assert m % t == 0
    return t


def kernel(x, norm1_w, w_in, gla_wg2, gla_bg, gla_norm_w, pool_w, pool_scale, moba_qn_w, moba_kn_w,
           rel_bias, w_up_a, w_up_b, w_up_c, w_out, norm2_w, ffn_w_gate, ffn_w_up, ffn_w_down):
    b, s, d = x.shape
    assert d == D_MODEL and s % MOBA_BLOCK == 0 and s % GLA_T == 0
    m = b * s
    bf = lambda a: a.astype(BF16)
    bias_tab = _bias_tables(rel_bias)
    w_in_bf = bf(w_in)
    norm1 = norm1_w.reshape(DEPTH, 1, d)
    qn = moba_qn_w.reshape(DEPTH, 1, MOBA_HEAD_DIM)
    kn = moba_kn_w.reshape(DEPTH, 1, MOBA_HEAD_DIM)
    mix_weights = (bf(pool_w), pool_scale.reshape(DEPTH, 1, -1), bf(w_up_a), bf(w_up_b), bf(w_up_c), bf(w_out),
                   norm2_w.reshape(DEPTH, 1, d), bf(ffn_w_gate), bf(ffn_w_up), bf(ffn_w_down))
    x2d = x.reshape(m, d)
    for l in range(DEPTH):
        proj2d, kmean = _inproj(x2d, norm1, w_in_bf, qn, kn, l, _row_tile(m, 512))
        proj = proj2d.reshape(b, s, PROJ_COLS)
        ya = _gla(proj, gla_wg2[l], gla_bg[l], gla_norm_w[l])
        yc = _moba(proj, kmean, rel_bias, bias_tab)
        x2d = _mix(x2d, ya.reshape(m, -1), yc.reshape(m, -1), proj2d, s, l, *mix_weights, _row_tile(s, 512))
    return x2d.reshape(b, s, d)
```

```python
import functools
import math

import numpy as np
import jax
import jax.numpy as jnp
from jax import lax
from jax.experimental import pallas as pl
from jax.experimental.pallas import tpu as pltpu

D_MODEL = 1024
DEPTH = 2
BRANCH_WIDTH = D_MODEL // 2
GLA_HEADS = 4
GLA_DV = BRANCH_WIDTH // GLA_HEADS
GLA_DK = GLA_DV // 2
GLA_LOW_RANK = 16
GLA_GATE_NORMALIZER = 16.0
POOL_WINDOWS = (2, 4, 8, 16)
POOL_GROUPS = 4
POOL_GROUP_DIM = BRANCH_WIDTH // POOL_GROUPS
MOBA_HEADS = 4
MOBA_HEAD_DIM = BRANCH_WIDTH // MOBA_HEADS
MOBA_BLOCK = 256
MOBA_TOPK = 3
REL_BUCKETS = 32
REL_MAX_EXACT = REL_BUCKETS // 2
REL_MAX_DIST = 128
D_FF = ((8 * D_MODEL + 767) // 768) * 256
NORM_EPS = 1e-6
NEG_INF = -1e30
LOG2E = math.log2(math.e)

LANES = 128
BF16_SUBLANES = 16
VMEM_LIMIT_BYTES = 56 * 1024 * 1024

_SECTIONS = (("gq", 256), ("gk", 256), ("gv", 512), ("glr", GLA_LOW_RANK), ("gr", 512), ("pu", 512),
             ("mq", 512), ("mk", 512), ("mv", 512), ("ga", 1024), ("gb", 1024), ("gc", 1024))
_OFF = {}
_o = 0
for _n, _w in _SECTIONS:
    if _n != "glr":
        assert _o % min(_w, 512) == 0
        _OFF[_n] = _o
        _o += _w
_OFF["glr"] = _o
PROJ_COLS = _o + LANES
GLR_SRC = sum(w for n, w in _SECTIONS[:3])
PROJ_CHUNK = 512

GLA_T = 128
GLA_LEVELS = 7
GLA_BATCH_PER_STEP = 4
MOBA_BATCH_PER_STEP = 2
POOL_HALO = 16
FFN_CHUNK = 256
FFN_NCHUNK = D_FF // FFN_CHUNK

F32 = jnp.float32
BF16 = jnp.bfloat16


def _params(*sem):
    return pltpu.CompilerParams(dimension_semantics=sem, vmem_limit_bytes=VMEM_LIMIT_BYTES)


def _dot(a, b):
    return jnp.dot(a, b, preferred_element_type=F32)


def _dot_nt(a, b):
    return lax.dot_general(a, b, (((1,), (1,)), ((), ())), preferred_element_type=F32)


def _split_hi_lo(x):
    hi = x.astype(BF16)
    lo = (x - hi.astype(F32)).astype(BF16)
    return hi, lo


def _rms(x, w):
    return x * lax.rsqrt(jnp.mean(x * x, axis=-1, keepdims=True) + NORM_EPS) * w


def _sigmoid(x):
    return 1.0 / (1.0 + jnp.exp(-x))


def _head_rms(y, w, scale=1.0):
    dh = MOBA_HEAD_DIM
    return jnp.concatenate([_rms(y[:, h * dh:(h + 1) * dh], w) * scale for h in range(y.shape[1] // dh)], axis=1)


def _inproj_kernel(x_ref, nw_ref, w_ref, qn_ref, kn_ref, o_ref, km_ref, wp_ref):
    @pl.when(pl.program_id(0) == 0)
    def _():
        glr_end = GLR_SRC + GLA_LOW_RANK
        wp_ref[:, :GLR_SRC] = w_ref[:, :GLR_SRC]
        wp_ref[:, GLR_SRC:_OFF["glr"]] = w_ref[:, glr_end:]
        wp_ref[:, _OFF["glr"]:] = jnp.concatenate(
            [w_ref[:, GLR_SRC:glr_end], jnp.zeros((D_MODEL, LANES - GLA_LOW_RANK), BF16)], axis=1)

    h = _rms(x_ref[...], nw_ref[...]).astype(BF16)
    for lo in range(0, PROJ_COLS, PROJ_CHUNK):
        cols = slice(lo, min(lo + PROJ_CHUNK, PROJ_COLS))
        y = _dot(h, wp_ref[:, cols])
        if lo == _OFF["gr"]:
            y = y * _sigmoid(y)
        elif lo == _OFF["mq"]:
            y = _head_rms(y, qn_ref[...], MOBA_HEAD_DIM ** -0.5 * LOG2E)
        elif lo == _OFF["mk"]:
            y = _head_rms(y, kn_ref[...])
            for j in range(y.shape[0] // MOBA_BLOCK):
                km_ref[j] = jnp.mean(y[j * MOBA_BLOCK:(j + 1) * MOBA_BLOCK], axis=0, keepdims=True)
        o_ref[:, cols] = y.astype(BF16)


def _inproj(x2d, norm_w, w_bf, qn_w, kn_w, layer, tm):
    m = x2d.shape[0]
    in_cols = w_bf.shape[-1]
    assert tm % MOBA_BLOCK == 0 and all(_OFF[n] % PROJ_CHUNK == 0 for n in ("gr", "mq", "mk"))
    layer_block = lambda *shape: pl.BlockSpec((None,) + shape, lambda i: (layer,) + (0,) * len(shape))
    return pl.pallas_call(
        _inproj_kernel,
        out_shape=(jax.ShapeDtypeStruct((m, PROJ_COLS), BF16),
                   jax.ShapeDtypeStruct((m // MOBA_BLOCK, 1, BRANCH_WIDTH), F32)),
        grid=(m // tm,),
        in_specs=[pl.BlockSpec((tm, D_MODEL), lambda i: (i, 0)),
                  layer_block(1, D_MODEL),
                  pl.BlockSpec((None, D_MODEL, in_cols), lambda i: (layer, 0, 0), pipeline_mode=pl.Buffered(1)),
                  layer_block(1, MOBA_HEAD_DIM), layer_block(1, MOBA_HEAD_DIM)],
        out_specs=(pl.BlockSpec((tm, PROJ_COLS), lambda i: (i, 0)),
                   pl.BlockSpec((tm // MOBA_BLOCK, 1, BRANCH_WIDTH), lambda i: (i, 0, 0))),
        scratch_shapes=[pltpu.VMEM((D_MODEL, PROJ_COLS), BF16)],
        compiler_params=_params("arbitrary"),
        name="inproj",
    )(x2d, norm_w, w_bf, qn_w, kn_w)


def _gla_constants():
    t = np.arange(GLA_T)
    u = np.arange(GLA_T)
    a_rows = []
    level = np.full((GLA_T, GLA_T), -1, np.int32)
    for lev in range(GLA_LEVELS):
        s = (GLA_T // 2) >> lev
        g = 2 * s
        first = (t % g) < s
        ref = (t // g) * g + s - 1
        a = np.where(first[:, None],
                     (u[None, :] > t[:, None]) & (u[None, :] <= ref[:, None]),
                     (u[None, :] > ref[:, None]) & (u[None, :] <= t[:, None]))
        a_rows.append(a)
        same = (t[:, None] // g) == (t[None, :] // g)
        level[same & (~first)[:, None] & first[None, :]] = lev
    a_rows.append(u[None, :] <= t[:, None])
    level[np.eye(GLA_T, dtype=bool)] = GLA_LEVELS
    a_all = np.concatenate(a_rows, axis=0).astype(np.float32)
    akt = np.concatenate([(u[:, None] > t[None, :]), np.ones((GLA_T, GLA_T), bool)], axis=1).astype(np.float32)
    return np.concatenate([a_all, a_all], axis=1), np.concatenate([akt, akt], axis=0), level


def _log_sigmoid(z):
    return jnp.minimum(z, 0.0) - jnp.log1p(jnp.exp(-jnp.abs(z)))


def _gla_kernel(q_ref, k_ref, v_ref, sr_ref, glr_ref, wg2_ref, bg_ref, a_ref, akt_ref, lvl_ref, nw_ref, o_ref,
                s_ref, sbd_ref):
    nh, dk, dv, t = GLA_HEADS, GLA_DK, GLA_DV, GLA_T

    @pl.when(pl.program_id(1) == 0)
    def _():
        s_ref[...] = jnp.zeros_like(s_ref)
        sbd_ref[...] = jnp.zeros_like(sbd_ref)

    rows = range(q_ref.shape[0])
    each = lambda f: [f(bi) for bi in rows]
    q8 = each(lambda bi: q_ref[bi] * (dk ** -0.5))
    k = each(lambda bi: k_ref[bi])
    v = each(lambda bi: v_ref[bi])
    s_old = each(lambda bi: [s_ref[bi, h] for h in range(nh)])

    inv = 1.0 / GLA_GATE_NORMALIZER
    glog = each(lambda bi: _log_sigmoid(_dot(glr_ref[bi], wg2_ref[...]) + bg_ref[...]) * inv)
    glog_t = each(lambda bi: glog[bi].T)

    dec_all = each(lambda bi: jnp.exp(_dot(a_ref[...], jnp.concatenate(_split_hi_lo(glog[bi]), axis=0))).astype(BF16))
    et_all = each(lambda bi: _dot(jnp.concatenate(_split_hi_lo(glog_t[bi]), axis=1), akt_ref[...]))

    qe = each(lambda bi: q8[bi] * dec_all[bi][GLA_LEVELS * t:(GLA_LEVELS + 1) * t])
    o_inter = each(lambda bi: _dot(qe[bi], sbd_ref[bi]))

    lane_head = lax.broadcasted_iota(jnp.int32, (t, nh * dk), 1) // dk
    head_masks = [lane_head == h for h in range(nh)]
    lvl = lvl_ref[...]
    attn = each(lambda bi: [jnp.zeros((t, t), F32) for _ in range(nh)])
    for lev in range(GLA_LEVELS + 1):
        m = lvl == lev
        for bi in rows:
            if lev < GLA_LEVELS:
                e = dec_all[bi][lev * t:(lev + 1) * t]
                ql = q8[bi] * e
                kl = k[bi] * e
            else:
                ql = q8[bi]
                kl = k[bi]
            kbd = jnp.concatenate([jnp.where(head_masks[h], kl, jnp.zeros_like(kl)) for h in range(nh)], axis=0)
            sc = _dot_nt(ql, kbd)
            for h in range(nh):
                attn[bi][h] = jnp.where(m, sc[:, h * t:(h + 1) * t], attn[bi][h])

    nw = nw_ref[...]
    for bi in rows:
        outs = []
        for h in range(nh):
            cols = slice(h * dv, (h + 1) * dv)
            outs.append(_rms(_dot(attn[bi][h].astype(BF16), v[bi][:, cols]) + o_inter[bi][:, cols], nw))
        o_ref[bi] = (jnp.concatenate(outs, axis=1) * sr_ref[bi].astype(F32)).astype(BF16)

    k_tail_t = each(lambda bi: (k[bi].astype(F32).T * jnp.exp(et_all[bi][:, :t])).astype(BF16))
    dec_t = each(lambda bi: jnp.exp(et_all[bi][:, t:]))
    for bi in rows:
        for h in range(nh):
            ch = slice(h * dk, (h + 1) * dk)
            s_new = dec_t[bi][ch, :] * s_old[bi][h] + _dot(k_tail_t[bi][ch, :], v[bi][:, h * dv:(h + 1) * dv])
            s_ref[bi, h] = s_new
            sbd_ref[bi, ch, h * dv:(h + 1) * dv] = s_new.astype(BF16)


def _gla(proj, wg2, bg, norm_w):
    b, s, _ = proj.shape
    t = GLA_T
    hk = GLA_HEADS * GLA_DK
    hv = GLA_HEADS * GLA_DV
    a_all, akt, level = _gla_constants()
    wg2_p = jnp.zeros((LANES, hk), F32).at[:GLA_LOW_RANK].set(wg2).astype(BF16)
    const = lambda *shape: pl.BlockSpec(shape, lambda i, j: (0,) * len(shape))
    nb = GLA_BATCH_PER_STEP if b % GLA_BATCH_PER_STEP == 0 else 1
    return pl.pallas_call(
        _gla_kernel,
        out_shape=jax.ShapeDtypeStruct((b, s, hv), BF16),
        grid=(b // nb, s // t),
        in_specs=[pl.BlockSpec((nb, t, hk), lambda i, j: (i, j, _OFF["gq"] // hk)),
                  pl.BlockSpec((nb, t, hk), lambda i, j: (i, j, _OFF["gk"] // hk)),
                  pl.BlockSpec((nb, t, hv), lambda i, j: (i, j, _OFF["gv"] // hv)),
                  pl.BlockSpec((nb, t, hv), lambda i, j: (i, j, _OFF["gr"] // hv)),
                  pl.BlockSpec((nb, t, LANES), lambda i, j: (i, j, _OFF["glr"] // LANES)),
                  const(LANES, hk), const(1, hk),
                  const((GLA_LEVELS + 1) * t, 2 * t), const(2 * t, 2 * t), const(t, t),
                  const(1, GLA_DV)],
        out_specs=pl.BlockSpec((nb, t, hv), lambda i, j: (i, j, 0)),
        scratch_shapes=[pltpu.VMEM((nb, GLA_HEADS, GLA_DK, GLA_DV), F32), pltpu.VMEM((nb, hk, hv), BF16)],
        compiler_params=_params("parallel", "arbitrary"),
        name="gla",
    )(proj, proj, proj, proj, proj, wg2_p, bg.reshape(1, hk),
      jnp.asarray(a_all, BF16), jnp.asarray(akt, BF16), jnp.asarray(level), norm_w.reshape(1, GLA_DV))


def _t5_bucket_np(dist):
    n = np.maximum(dist, 0)
    large = REL_MAX_EXACT + (np.log(np.maximum(n, 1).astype(np.float32) / REL_MAX_EXACT)
                             / math.log(REL_MAX_DIST / REL_MAX_EXACT)
                             * (REL_BUCKETS - REL_MAX_EXACT)).astype(np.int32)
    large = np.minimum(large, REL_BUCKETS - 1)
    return np.where(n < REL_MAX_EXACT, n, large).astype(np.int32)


def _bias_kernel(rb_ref, bk_ref, o_ref):
    h = pl.program_id(0)
    for which in range(2):
        bk = bk_ref[which]
        acc = jnp.full(bk.shape, NEG_INF, F32)
        for bucket in range(REL_BUCKETS):
            acc = jnp.where(bk == bucket, rb_ref[bucket, h] * LOG2E, acc)
        o_ref[0, which] = acc


def _bias_tables(rel_bias):
    blk = MOBA_BLOCK
    j = np.arange(blk)[:, None]
    i = np.arange(blk)[None, :]
    own = np.where(i >= j, _t5_bucket_np(i - j), -1)
    prev = _t5_bucket_np(i + blk - j)
    buckets = np.stack([own, prev]).astype(np.int32)
    return pl.pallas_call(
        _bias_kernel,
        out_shape=jax.ShapeDtypeStruct((MOBA_HEADS, 2, blk, blk), F32),
        grid=(MOBA_HEADS,),
        in_specs=[pl.BlockSpec(memory_space=pltpu.SMEM),
                  pl.BlockSpec((2, blk, blk), lambda h: (0, 0, 0))],
        out_specs=pl.BlockSpec((1, 2, blk, blk), lambda h: (h, 0, 0, 0)),
        compiler_params=_params("parallel"),
        name="moba_bias",
    )(rel_bias, jnp.asarray(buckets))


def _moba_kernel(rb_ref, q_ref, k_ref, v_ref, km_ref, bias_ref, o_ref, vt_ref, sel_ref):
    blk, dh, nh = MOBA_BLOCK, MOBA_HEAD_DIM, MOBA_HEADS
    nb = k_ref.shape[1] // blk
    qb = pl.program_id(1)
    chains = [(bi, h) for bi in range(q_ref.shape[0]) for h in range(nh)]
    each = lambda f: [f(c, bi, h) for c, (bi, h) in enumerate(chains)]
    head = lambda h: slice(h * dh, (h + 1) * dh)
    keys = lambda bi, h, n: k_ref[bi, pl.ds(pl.multiple_of(n * blk, blk), blk), head(h)]

    @pl.when(qb == 0)
    def _():
        ones = jnp.ones((BF16_SUBLANES, blk), BF16)
        for c, (bi, h) in enumerate(chains):
            for n in range(nb):
                vt = v_ref[bi, n * blk:(n + 1) * blk, head(h)].astype(F32).T.astype(BF16)
                vt_ref[c, n] = jnp.concatenate([vt, ones], axis=0)

    n_iota = lax.broadcasted_iota(jnp.int32, (nb, blk), 0)
    q_sc = each(lambda c, bi, h: q_ref[bi, :, head(h)])

    s_own = each(lambda c, bi, h: _dot_nt(keys(bi, h, qb), q_sc[c]) + bias_ref[h, 0])
    m0 = each(lambda c, bi, h: jnp.max(s_own[c], axis=0, keepdims=True))
    p0 = each(lambda c, bi, h: jnp.exp2(s_own[c] - m0[c]))
    carry = each(lambda c, bi, h: (m0[c], _dot(vt_ref[c, qb], p0[c].astype(BF16))))

    for c, (bi, h) in enumerate(chains):
        km_hi, km_lo = _split_hi_lo(km_ref[bi * nb:(bi + 1) * nb, 0, head(h)])
        sc = _dot_nt(km_hi, q_sc[c]) + _dot_nt(km_lo, q_sc[c])
        rank = jnp.zeros((nb, blk), jnp.int32)
        for m in range(nb):
            sm = sc[m:m + 1, :]
            beats = ((sm > sc) | ((sm == sc) & (m < n_iota))) & (m < qb)
            rank = rank + beats.astype(jnp.int32)
        sel_ref[c] = jnp.where((n_iota < qb) & (rank < MOBA_TOPK), 0.0, NEG_INF)

    def attend(n, bias_of, carry):
        s = each(lambda c, bi, h: _dot_nt(keys(bi, h, n), q_sc[c]) + bias_of(c, h))
        m_new = each(lambda c, bi, h: jnp.maximum(carry[c][0], jnp.max(s[c], axis=0, keepdims=True)))
        alpha = each(lambda c, bi, h: jnp.exp2(carry[c][0] - m_new[c]))
        p = each(lambda c, bi, h: jnp.exp2(s[c] - m_new[c]))
        pv = each(lambda c, bi, h: _dot(vt_ref[c, n], p[c].astype(BF16)))
        return each(lambda c, bi, h: (m_new[c], alpha[c] * carry[c][1] + pv[c]))

    prev = jnp.maximum(qb - 1, 0)
    carry = attend(prev, lambda c, h: bias_ref[h, 1] + sel_ref[c, pl.ds(prev, 1), :], carry)
    carry = lax.fori_loop(
        0, prev,
        lambda n, cr: attend(n, lambda c, h: sel_ref[c, pl.ds(n, 1), :] + rb_ref[REL_BUCKETS - 1, h] * LOG2E, cr), carry)
    outs = each(lambda c, bi, h: (carry[c][1][:dh] / carry[c][1][dh:dh + 1]).T)
    for bi in range(q_ref.shape[0]):
        o_ref[bi] = jnp.concatenate(outs[bi * nh:(bi + 1) * nh], axis=1).astype(BF16)


def _moba(proj, kmean, rel_bias, bias_tab):
    b, s, _ = proj.shape
    blk, dh, nh = MOBA_BLOCK, MOBA_HEAD_DIM, MOBA_HEADS
    c = nh * dh
    nb = s // blk
    rows = MOBA_BATCH_PER_STEP if b % MOBA_BATCH_PER_STEP == 0 else 1
    nc = rows * nh
    return pl.pallas_call(
        _moba_kernel,
        out_shape=jax.ShapeDtypeStruct((b, s, c), BF16),
        grid=(b // rows, nb),
        in_specs=[pl.BlockSpec(memory_space=pltpu.SMEM),
                  pl.BlockSpec((rows, blk, c), lambda i, j: (i, j, _OFF["mq"] // c)),
                  pl.BlockSpec((rows, s, c), lambda i, j: (i, 0, _OFF["mk"] // c)),
                  pl.BlockSpec((rows, s, c), lambda i, j: (i, 0, _OFF["mv"] // c)),
                  pl.BlockSpec((rows * nb, 1, c), lambda i, j: (i, 0, 0)),
                  pl.BlockSpec((nh, 2, blk, blk), lambda i, j: (0, 0, 0, 0))],
        out_specs=pl.BlockSpec((rows, blk, c), lambda i, j: (i, j, 0)),
        scratch_shapes=[pltpu.VMEM((nc, nb, dh + BF16_SUBLANES, blk), BF16), pltpu.VMEM((nc, nb, blk), F32)],
        compiler_params=_params("parallel", "arbitrary"),
        name="moba",
    )(rel_bias, proj, proj, proj, kmean, bias_tab)


def _pool_branch(cur, halo, pw_ref, ps_ref, start):
    tm = cur.shape[0]
    ext = jnp.concatenate([halo, cur], axis=0)
    pos = start + lax.broadcasted_iota(jnp.int32, (tm, 1), 0)
    outs = []
    for g, w in enumerate(POOL_WINDOWS):
        cols = slice(g * POOL_GROUP_DIM, (g + 1) * POOL_GROUP_DIM)
        a = ext[:, cols]
        shift = 1
        while shift < w:
            a = a + pltpu.roll(a, shift, 0)
            shift *= 2
        cnt = jnp.minimum(pos + 1, w).astype(F32)
        p = a[POOL_HALO:, :] / cnt - cur[:, cols]
        outs.append(_dot(p.astype(BF16), pw_ref[g]))
    return jnp.concatenate(outs, axis=1) * ps_ref[...]


def _mix_kernel(seq, x_ref, ya_ref, yc_ref, u_ref, halo_ref, ga0_ref, ga1_ref, gb0_ref, gb1_ref, gc0_ref, gc1_ref,
                pw_ref, ps_ref, wa_ref, wb_ref, wc_ref, wo_ref, n2_ref, wg_ref, wu_ref, wd_ref, o_ref):
    tm = x_ref.shape[0]
    start = (pl.program_id(0) * tm) % seq
    halo = jnp.where(start > 0, halo_ref[...].astype(F32), 0.0)
    yb = _pool_branch(u_ref[...].astype(F32), halo, pw_ref, ps_ref, start)
    gate = lambda lo_ref, hi_ref: _sigmoid(jnp.concatenate([lo_ref[...], hi_ref[...]], axis=1).astype(F32))
    merged = (gate(ga0_ref, ga1_ref) * _dot(ya_ref[...], wa_ref[...])
              + gate(gb0_ref, gb1_ref) * _dot(yb.astype(BF16), wb_ref[...])
              + gate(gc0_ref, gc1_ref) * _dot(yc_ref[...], wc_ref[...]))
    x1 = x_ref[...] + _dot(merged.astype(BF16), wo_ref[...])
    h = _rms(x1, n2_ref[...]).astype(BF16)
    acc = x1
    for c in range(FFN_NCHUNK):
        cols = slice(c * FFN_CHUNK, (c + 1) * FFN_CHUNK)
        g = _dot(h, wg_ref[:, cols])
        u = _dot(h, wu_ref[:, cols])
        act = (g * _sigmoid(g) * u).astype(BF16)
        acc = acc + _dot(act, wd_ref[cols, :])
    o_ref[...] = acc


def _mix(x2d, ya, yc, proj2d, seq, layer, pool_w, pool_scale, wa, wb, wc, wo, norm2_w, wg, wu, wd, tm):
    m = x2d.shape[0]
    c = BRANCH_WIDTH
    assert seq % tm == 0 and tm % POOL_HALO == 0 and POOL_HALO == BF16_SUBLANES
    row = lambda w, col: pl.BlockSpec((tm, w), lambda i: (i, col))
    resident = lambda *shape: pl.BlockSpec((None,) + shape, lambda i: (layer,) + (0,) * len(shape),
                                           pipeline_mode=pl.Buffered(1))
    halo_blocks = tm // POOL_HALO
    gates = [row(c, _OFF[g] // c + half) for g in ("ga", "gb", "gc") for half in range(D_MODEL // c)]
    return pl.pallas_call(
        functools.partial(_mix_kernel, seq),
        out_shape=jax.ShapeDtypeStruct((m, D_MODEL), F32),
        grid=(m // tm,),
        in_specs=[row(D_MODEL, 0), row(c, 0), row(c, 0), row(c, _OFF["pu"] // c),
                  pl.BlockSpec((POOL_HALO, c), lambda i: (jnp.maximum(i * halo_blocks - 1, 0), _OFF["pu"] // c)),
                  *gates,
                  resident(POOL_GROUPS, POOL_GROUP_DIM, POOL_GROUP_DIM), resident(1, c),
                  resident(c, D_MODEL), resident(c, D_MODEL), resident(c, D_MODEL), resident(D_MODEL, D_MODEL),
                  resident(1, D_MODEL), resident(D_MODEL, D_FF), resident(D_MODEL, D_FF), resident(D_FF, D_MODEL)],
        out_specs=row(D_MODEL, 0),
        compiler_params=_params("parallel"),
        name="mix",
    )(x2d, ya, yc, proj2d, proj2d, *([proj2d] * len(gates)), pool_w, pool_scale,
      wa, wb, wc, wo, norm2_w, wg, wu, wd)


def _row_tile(m, want):
    t = min(m, want)
    assert m % t == 0
    return t


def kernel(x, norm1_w, w_in, gla_wg2, gla_bg, gla_norm_w, pool_w, pool_scale, moba_qn_w, moba_kn_w,
           rel_bias, w_up_a, w_up_b, w_up_c, w_out, norm2_w, ffn_w_gate, ffn_w_up, ffn_w_down):
    b, s, d = x.shape
    assert d == D_MODEL and s % MOBA_BLOCK == 0 and s % GLA_T == 0
    m = b * s
    bf = lambda a: a.astype(BF16)
    bias_tab = _bias_tables(rel_bias)
    w_in_bf = bf(w_in)
    norm1 = norm1_w.reshape(DEPTH, 1, d)
    qn = moba_qn_w.reshape(DEPTH, 1, MOBA_HEAD_DIM)
    kn = moba_kn_w.reshape(DEPTH, 1, MOBA_HEAD_DIM)
    mix_weights = (bf(pool_w), pool_scale.reshape(DEPTH, 1, -1), bf(w_up_a), bf(w_up_b), bf(w_up_c), bf(w_out),
                   norm2_w.reshape(DEPTH, 1, d), bf(ffn_w_gate), bf(ffn_w_up), bf(ffn_w_down))
    x2d = x.reshape(m, d)
    for l in range(DEPTH):
        proj2d, kmean = _inproj(x2d, norm1, w_in_bf, qn, kn, l, _row_tile(m, 512))
        proj = proj2d.reshape(b, s, PROJ_COLS)
        ya = _gla(proj, gla_wg2[l], gla_bg[l], gla_norm_w[l])
        yc = _moba(proj, kmean, rel_bias, bias_tab)
        x2d = _mix(x2d, ya.reshape(m, -1), yc.reshape(m, -1), proj2d, s, l, *mix_weights, _row_tile(s, 512))
    return x2d.reshape(b, s, d)
```

```python
import functools
import math

import numpy as np
import jax
import jax.numpy as jnp
from jax import lax
from jax.experimental import pallas as pl
from jax.experimental.pallas import tpu as pltpu

D_MODEL = 1024
DEPTH = 2
BRANCH_WIDTH = D_MODEL // 2
GLA_HEADS = 4
GLA_DV = BRANCH_WIDTH // GLA_HEADS
GLA_DK = GLA_DV // 2
GLA_LOW_RANK = 16
GLA_GATE_NORMALIZER = 16.0
POOL_WINDOWS = (2, 4, 8, 16)
POOL_GROUPS = 4
POOL_GROUP_DIM = BRANCH_WIDTH // POOL_GROUPS
MOBA_HEADS = 4
MOBA_HEAD_DIM = BRANCH_WIDTH // MOBA_HEADS
MOBA_BLOCK = 256
MOBA_TOPK = 3
REL_BUCKETS = 32
REL_MAX_EXACT = REL_BUCKETS // 2
REL_MAX_DIST = 128
D_FF = ((8 * D_MODEL + 767) // 768) * 256
NORM_EPS = 1e-6
NEG_INF = -1e30
LOG2E = math.log2(math.e)

LANES = 128
BF16_SUBLANES = 16
VMEM_LIMIT_BYTES = 56 * 1024 * 1024

_SECTIONS = (("gq", 256), ("gk", 256), ("gv", 512), ("glr", GLA_LOW_RANK), ("gr", 512), ("pu", 512),
             ("mq", 512), ("mk", 512), ("mv", 512), ("ga", 1024), ("gb", 1024), ("gc", 1024))
_OFF = {}
_o = 0
for _n, _w in _SECTIONS:
    if _n != "glr":
        assert _o % min(_w, 512) == 0
        _OFF[_n] = _o
        _o += _w
_OFF["glr"] = _o
PROJ_COLS = _o + LANES
GLR_SRC = sum(w for n, w in _SECTIONS[:3])
PROJ_CHUNK = 512

GLA_T = 128
GLA_LEVELS = 7
GLA_BATCH_PER_STEP = 4
MOBA_BATCH_PER_STEP = 4
POOL_HALO = 16
FFN_CHUNK = 256
FFN_NCHUNK = D_FF // FFN_CHUNK

F32 = jnp.float32
BF16 = jnp.bfloat16


def _params(*sem):
    return pltpu.CompilerParams(dimension_semantics=sem, vmem_limit_bytes=VMEM_LIMIT_BYTES)


def _dot(a, b):
    return jnp.dot(a, b, preferred_element_type=F32)


def _dot_nt(a, b):
    return lax.dot_general(a, b, (((1,), (1,)), ((), ())), preferred_element_type=F32)


def _split_hi_lo(x):
    hi = x.astype(BF16)
    lo = (x - hi.astype(F32)).astype(BF16)
    return hi, lo


def _rms(x, w):
    return x * lax.rsqrt(jnp.mean(x * x, axis=-1, keepdims=True) + NORM_EPS) * w


def _sigmoid(x):
    return 1.0 / (1.0 + jnp.exp(-x))


def _head_rms(y, w, scale=1.0):
    dh = MOBA_HEAD_DIM
    return jnp.concatenate([_rms(y[:, h * dh:(h + 1) * dh], w) * scale for h in range(y.shape[1] // dh)], axis=1)


def _inproj_kernel(x_ref, nw_ref, w_ref, qn_ref, kn_ref, o_ref, km_ref, wp_ref):
    @pl.when(pl.program_id(0) == 0)
    def _():
        glr_end = GLR_SRC + GLA_LOW_RANK
        wp_ref[:, :GLR_SRC] = w_ref[:, :GLR_SRC]
        wp_ref[:, GLR_SRC:_OFF["glr"]] = w_ref[:, glr_end:]
        wp_ref[:, _OFF["glr"]:] = jnp.concatenate(
            [w_ref[:, GLR_SRC:glr_end], jnp.zeros((D_MODEL, LANES - GLA_LOW_RANK), BF16)], axis=1)

    h = _rms(x_ref[...], nw_ref[...]).astype(BF16)
    for lo in range(0, PROJ_COLS, PROJ_CHUNK):
        cols = slice(lo, min(lo + PROJ_CHUNK, PROJ_COLS))
        y = _dot(h, wp_ref[:, cols])
        if lo == _OFF["gr"]:
            y = y * _sigmoid(y)
        elif lo == _OFF["mq"]:
            y = _head_rms(y, qn_ref[...], MOBA_HEAD_DIM ** -0.5 * LOG2E)
        elif lo == _OFF["mk"]:
            y = _head_rms(y, kn_ref[...])
            for j in range(y.shape[0] // MOBA_BLOCK):
                km_ref[j] = jnp.mean(y[j * MOBA_BLOCK:(j + 1) * MOBA_BLOCK], axis=0, keepdims=True)
        o_ref[:, cols] = y.astype(BF16)


def _inproj(x2d, norm_w, w_bf, qn_w, kn_w, layer, tm):
    m = x2d.shape[0]
    in_cols = w_bf.shape[-1]
    assert tm % MOBA_BLOCK == 0 and all(_OFF[n] % PROJ_CHUNK == 0 for n in ("gr", "mq", "mk"))
    layer_block = lambda *shape: pl.BlockSpec((None,) + shape, lambda i: (layer,) + (0,) * len(shape))
    return pl.pallas_call(
        _inproj_kernel,
        out_shape=(jax.ShapeDtypeStruct((m, PROJ_COLS), BF16),
                   jax.ShapeDtypeStruct((m // MOBA_BLOCK, 1, BRANCH_WIDTH), F32)),
        grid=(m // tm,),
        in_specs=[pl.BlockSpec((tm, D_MODEL), lambda i: (i, 0)),
                  layer_block(1, D_MODEL),
                  pl.BlockSpec((None, D_MODEL, in_cols), lambda i: (layer, 0, 0), pipeline_mode=pl.Buffered(1)),
                  layer_block(1, MOBA_HEAD_DIM), layer_block(1, MOBA_HEAD_DIM)],
        out_specs=(pl.BlockSpec((tm, PROJ_COLS), lambda i: (i, 0)),
                   pl.BlockSpec((tm // MOBA_BLOCK, 1, BRANCH_WIDTH), lambda i: (i, 0, 0))),
        scratch_shapes=[pltpu.VMEM((D_MODEL, PROJ_COLS), BF16)],
        compiler_params=_params("arbitrary"),
        name="inproj",
    )(x2d, norm_w, w_bf, qn_w, kn_w)


def _gla_constants():
    t = np.arange(GLA_T)
    u = np.arange(GLA_T)
    a_rows = []
    level = np.full((GLA_T, GLA_T), -1, np.int32)
    for lev in range(GLA_LEVELS):
        s = (GLA_T // 2) >> lev
        g = 2 * s
        first = (t % g) < s
        ref = (t // g) * g + s - 1
        a = np.where(first[:, None],
                     (u[None, :] > t[:, None]) & (u[None, :] <= ref[:, None]),
                     (u[None, :] > ref[:, None]) & (u[None, :] <= t[:, None]))
        a_rows.append(a)
        same = (t[:, None] // g) == (t[None, :] // g)
        level[same & (~first)[:, None] & first[None, :]] = lev
    a_rows.append(u[None, :] <= t[:, None])
    level[np.eye(GLA_T, dtype=bool)] = GLA_LEVELS
    a_all = np.concatenate(a_rows, axis=0).astype(np.float32)
    akt = np.concatenate([(u[:, None] > t[None, :]), np.ones((GLA_T, GLA_T), bool)], axis=1).astype(np.float32)
    return np.concatenate([a_all, a_all], axis=1), np.concatenate([akt, akt], axis=0), level


def _log_sigmoid(z):
    return jnp.minimum(z, 0.0) - jnp.log1p(jnp.exp(-jnp.abs(z)))


def _gla_kernel(q_ref, k_ref, v_ref, sr_ref, glr_ref, wg2_ref, bg_ref, a_ref, akt_ref, lvl_ref, nw_ref, o_ref,
                s_ref, sbd_ref):
    nh, dk, dv, t = GLA_HEADS, GLA_DK, GLA_DV, GLA_T

    @pl.when(pl.program_id(1) == 0)
    def _():
        s_ref[...] = jnp.zeros_like(s_ref)
        sbd_ref[...] = jnp.zeros_like(sbd_ref)

    rows = range(q_ref.shape[0])
    each = lambda f: [f(bi) for bi in rows]
    q8 = each(lambda bi: q_ref[bi] * (dk ** -0.5))
    k = each(lambda bi: k_ref[bi])
    v = each(lambda bi: v_ref[bi])
    s_old = each(lambda bi: [s_ref[bi, h] for h in range(nh)])

    inv = 1.0 / GLA_GATE_NORMALIZER
    glog = each(lambda bi: _log_sigmoid(_dot(glr_ref[bi], wg2_ref[...]) + bg_ref[...]) * inv)
    glog_t = each(lambda bi: glog[bi].T)

    dec_all = each(lambda bi: jnp.exp(_dot(a_ref[...], jnp.concatenate(_split_hi_lo(glog[bi]), axis=0))).astype(BF16))
    et_all = each(lambda bi: _dot(jnp.concatenate(_split_hi_lo(glog_t[bi]), axis=1), akt_ref[...]))

    qe = each(lambda bi: q8[bi] * dec_all[bi][GLA_LEVELS * t:(GLA_LEVELS + 1) * t])
    groups = range(nh * dk // LANES)
    o_inter = each(lambda bi: jnp.concatenate(
        [_dot(qe[bi][:, g * LANES:(g + 1) * LANES], sbd_ref[bi, g]) for g in groups], axis=1))

    per_group = LANES // dk
    lane_head = lax.broadcasted_iota(jnp.int32, (t, LANES), 1) // dk
    head_masks = [lane_head == h for h in range(per_group)]
    lvl = lvl_ref[...]
    attn = each(lambda bi: [jnp.zeros((t, t), F32) for _ in range(nh)])
    for lev in range(GLA_LEVELS + 1):
        m = lvl == lev
        for bi in rows:
            if lev < GLA_LEVELS:
                e = dec_all[bi][lev * t:(lev + 1) * t]
                ql = q8[bi] * e
                kl = k[bi] * e
            else:
                ql = q8[bi]
                kl = k[bi]
            for g in range(nh // per_group):
                lanes = slice(g * LANES, (g + 1) * LANES)
                klg = kl[:, lanes]
                kbd = jnp.concatenate([jnp.where(hm, klg, jnp.zeros_like(klg)) for hm in head_masks], axis=0)
                sc = _dot_nt(ql[:, lanes], kbd)
                for j in range(per_group):
                    h = g * per_group + j
                    attn[bi][h] = jnp.where(m, sc[:, j * t:(j + 1) * t], attn[bi][h])

    nw = nw_ref[...]
    for bi in rows:
        outs = []
        for h in range(nh):
            cols = slice(h * dv, (h + 1) * dv)
            outs.append(_rms(_dot(attn[bi][h].astype(BF16), v[bi][:, cols]) + o_inter[bi][:, cols], nw))
        o_ref[bi] = (jnp.concatenate(outs, axis=1) * sr_ref[bi].astype(F32)).astype(BF16)

    k_tail_t = each(lambda bi: (k[bi].astype(F32).T * jnp.exp(et_all[bi][:, :t])).astype(BF16))
    dec_t = each(lambda bi: jnp.exp(et_all[bi][:, t:]))
    for bi in rows:
        for h in range(nh):
            ch = slice(h * dk, (h + 1) * dk)
            s_new = dec_t[bi][ch, :] * s_old[bi][h] + _dot(k_tail_t[bi][ch, :], v[bi][:, h * dv:(h + 1) * dv])
            s_ref[bi, h] = s_new
            j = h % (LANES // dk)
            sbd_ref[bi, h * dk // LANES, j * dk:(j + 1) * dk, j * dv:(j + 1) * dv] = s_new.astype(BF16)


def _gla(proj, wg2, bg, norm_w):
    b, s, _ = proj.shape
    t = GLA_T
    hk = GLA_HEADS * GLA_DK
    hv = GLA_HEADS * GLA_DV
    a_all, akt, level = _gla_constants()
    wg2_p = jnp.zeros((LANES, hk), F32).at[:GLA_LOW_RANK].set(wg2).astype(BF16)
    const = lambda *shape: pl.BlockSpec(shape, lambda i, j: (0,) * len(shape))
    nb = GLA_BATCH_PER_STEP if b % GLA_BATCH_PER_STEP == 0 else 1
    return pl.pallas_call(
        _gla_kernel,
        out_shape=jax.ShapeDtypeStruct((b, s, hv), BF16),
        grid=(b // nb, s // t),
        in_specs=[pl.BlockSpec((nb, t, hk), lambda i, j: (i, j, _OFF["gq"] // hk)),
                  pl.BlockSpec((nb, t, hk), lambda i, j: (i, j, _OFF["gk"] // hk)),
                  pl.BlockSpec((nb, t, hv), lambda i, j: (i, j, _OFF["gv"] // hv)),
                  pl.BlockSpec((nb, t, hv), lambda i, j: (i, j, _OFF["gr"] // hv)),
                  pl.BlockSpec((nb, t, LANES), lambda i, j: (i, j, _OFF["glr"] // LANES)),
                  const(LANES, hk), const(1, hk),
                  const((GLA_LEVELS + 1) * t, 2 * t), const(2 * t, 2 * t), const(t, t),
                  const(1, GLA_DV)],
        out_specs=pl.BlockSpec((nb, t, hv), lambda i, j: (i, j, 0)),
        scratch_shapes=[pltpu.VMEM((nb, GLA_HEADS, GLA_DK, GLA_DV), F32),
                        pltpu.VMEM((nb, hk // LANES, LANES, LANES // GLA_DK * GLA_DV), BF16)],
        compiler_params=_params("parallel", "arbitrary"),
        name="gla",
    )(proj, proj, proj, proj, proj, wg2_p, bg.reshape(1, hk),
      jnp.asarray(a_all, BF16), jnp.asarray(akt, BF16), jnp.asarray(level), norm_w.reshape(1, GLA_DV))


def _t5_bucket_np(dist):
    n = np.maximum(dist, 0)
    large = REL_MAX_EXACT + (np.log(np.maximum(n, 1).astype(np.float32) / REL_MAX_EXACT)
                             / math.log(REL_MAX_DIST / REL_MAX_EXACT)
                             * (REL_BUCKETS - REL_MAX_EXACT)).astype(np.int32)
    large = np.minimum(large, REL_BUCKETS - 1)
    return np.where(n < REL_MAX_EXACT, n, large).astype(np.int32)


def _bias_kernel(rb_ref, bk_ref, o_ref):
    h = pl.program_id(0)
    for which in range(2):
        bk = bk_ref[which]
        acc = jnp.full(bk.shape, NEG_INF, F32)
        for bucket in range(REL_BUCKETS):
            acc = jnp.where(bk == bucket, rb_ref[bucket, h] * LOG2E, acc)
        o_ref[0, which] = acc


def _bias_tables(rel_bias):
    blk = MOBA_BLOCK
    j = np.arange(blk)[:, None]
    i = np.arange(blk)[None, :]
    own = np.where(i >= j, _t5_bucket_np(i - j), -1)
    prev = _t5_bucket_np(i + blk - j)
    buckets = np.stack([own, prev]).astype(np.int32)
    return pl.pallas_call(
        _bias_kernel,
        out_shape=jax.ShapeDtypeStruct((MOBA_HEADS, 2, blk, blk), F32),
        grid=(MOBA_HEADS,),
        in_specs=[pl.BlockSpec(memory_space=pltpu.SMEM),
                  pl.BlockSpec((2, blk, blk), lambda h: (0, 0, 0))],
        out_specs=pl.BlockSpec((1, 2, blk, blk), lambda h: (h, 0, 0, 0)),
        compiler_params=_params("parallel"),
        name="moba_bias",
    )(rel_bias, jnp.asarray(buckets))


def _moba_kernel(rb_ref, q_ref, k_ref, v_ref, km_ref, bias_ref, o_ref, vt_ref, sel_ref):
    blk, dh, nh = MOBA_BLOCK, MOBA_HEAD_DIM, MOBA_HEADS
    nb = k_ref.shape[1] // blk
    qb = pl.program_id(1)
    chains = [(bi, h) for bi in range(q_ref.shape[0]) for h in range(nh)]
    each = lambda f: [f(c, bi, h) for c, (bi, h) in enumerate(chains)]
    head = lambda h: slice(h * dh, (h + 1) * dh)
    keys = lambda bi, h, n: k_ref[bi, pl.ds(pl.multiple_of(n * blk, blk), blk), head(h)]

    @pl.when(qb == 0)
    def _():
        ones = jnp.ones((BF16_SUBLANES, blk), BF16)
        for c, (bi, h) in enumerate(chains):
            for n in range(nb):
                vt = v_ref[bi, n * blk:(n + 1) * blk, head(h)].astype(F32).T.astype(BF16)
                vt_ref[c, n] = jnp.concatenate([vt, ones], axis=0)

    n_iota = lax.broadcasted_iota(jnp.int32, (nb, blk), 0)
    q_sc = each(lambda c, bi, h: q_ref[bi, :, head(h)])

    s_own = each(lambda c, bi, h: _dot_nt(keys(bi, h, qb), q_sc[c]) + bias_ref[h, 0])
    m0 = each(lambda c, bi, h: jnp.max(s_own[c], axis=0, keepdims=True))
    p0 = each(lambda c, bi, h: jnp.exp2(s_own[c] - m0[c]))
    carry = each(lambda c, bi, h: (m0[c], _dot(vt_ref[c, qb], p0[c].astype(BF16))))

    for c, (bi, h) in enumerate(chains):
        km_hi, km_lo = _split_hi_lo(km_ref[bi * nb:(bi + 1) * nb, 0, head(h)])
        sc = _dot_nt(km_hi, q_sc[c]) + _dot_nt(km_lo, q_sc[c])
        rank = jnp.zeros((nb, blk), jnp.int32)
        for m in range(nb):
            sm = sc[m:m + 1, :]
            beats = ((sm > sc) | ((sm == sc) & (m < n_iota))) & (m < qb)
            rank = rank + beats.astype(jnp.int32)
        sel_ref[c] = jnp.where((n_iota < qb) & (rank < MOBA_TOPK), 0.0, NEG_INF)

    def attend(n, bias_of, carry):
        s = each(lambda c, bi, h: _dot_nt(keys(bi, h, n), q_sc[c]) + bias_of(c, h))
        m_new = each(lambda c, bi, h: jnp.maximum(carry[c][0], jnp.max(s[c], axis=0, keepdims=True)))
        alpha = each(lambda c, bi, h: jnp.exp2(carry[c][0] - m_new[c]))
        p = each(lambda c, bi, h: jnp.exp2(s[c] - m_new[c]))
        pv = each(lambda c, bi, h: _dot(vt_ref[c, n], p[c].astype(BF16)))
        return each(lambda c, bi, h: (m_new[c], alpha[c] * carry[c][1] + pv[c]))

    prev = jnp.maximum(qb - 1, 0)
    carry = attend(prev, lambda c, h: bias_ref[h, 1] + sel_ref[c, pl.ds(prev, 1), :], carry)
    carry = lax.fori_loop(
        0, prev,
        lambda n, cr: attend(n, lambda c, h: sel_ref[c, pl.ds(n, 1), :] + rb_ref[REL_BUCKETS - 1, h] * LOG2E, cr), carry)
    outs = each(lambda c, bi, h: (carry[c][1][:dh] / carry[c][1][dh:dh + 1]).T)
    for bi in range(q_ref.shape[0]):
        o_ref[bi] = jnp.concatenate(outs[bi * nh:(bi + 1) * nh], axis=1).astype(BF16)


def _moba(proj, kmean, rel_bias, bias_tab):
    b, s, _ = proj.shape
    blk, dh, nh = MOBA_BLOCK, MOBA_HEAD_DIM, MOBA_HEADS
    c = nh * dh
    nb = s // blk
    rows = MOBA_BATCH_PER_STEP if b % MOBA_BATCH_PER_STEP == 0 else 1
    nc = rows * nh
    return pl.pallas_call(
        _moba_kernel,
        out_shape=jax.ShapeDtypeStruct((b, s, c), BF16),
        grid=(b // rows, nb),
        in_specs=[pl.BlockSpec(memory_space=pltpu.SMEM),
                  pl.BlockSpec((rows, blk, c), lambda i, j: (i, j, _OFF["mq"] // c)),
                  pl.BlockSpec((rows, s, c), lambda i, j: (i, 0, _OFF["mk"] // c), pipeline_mode=pl.Buffered(1)),
                  pl.BlockSpec((rows, s, c), lambda i, j: (i, 0, _OFF["mv"] // c), pipeline_mode=pl.Buffered(1)),
                  pl.BlockSpec((rows * nb, 1, c), lambda i, j: (i, 0, 0)),
                  pl.BlockSpec((nh, 2, blk, blk), lambda i, j: (0, 0, 0, 0))],
        out_specs=pl.BlockSpec((rows, blk, c), lambda i, j: (i, j, 0)),
        scratch_shapes=[pltpu.VMEM((nc, nb, dh + BF16_SUBLANES, blk), BF16), pltpu.VMEM((nc, nb, blk), F32)],
        compiler_params=_params("parallel", "arbitrary"),
        name="moba",
    )(rel_bias, proj, proj, proj, kmean, bias_tab)


def _pool_branch(cur, halo, pw_ref, ps_ref, start):
    tm = cur.shape[0]
    ext = jnp.concatenate([halo, cur], axis=0)
    pos = start + lax.broadcasted_iota(jnp.int32, (tm, 1), 0)
    outs = []
    for g, w in enumerate(POOL_WINDOWS):
        cols = slice(g * POOL_GROUP_DIM, (g + 1) * POOL_GROUP_DIM)
        a = ext[:, cols]
        shift = 1
        while shift < w:
            a = a + pltpu.roll(a, shift, 0)
            shift *= 2
        cnt = jnp.minimum(pos + 1, w).astype(F32)
        p = a[POOL_HALO:, :] / cnt - cur[:, cols]
        outs.append(_dot(p.astype(BF16), pw_ref[g]))
    return jnp.concatenate(outs, axis=1) * ps_ref[...]


def _mix_kernel(seq, x_ref, ya_ref, yc_ref, u_ref, halo_ref, ga0_ref, ga1_ref, gb0_ref, gb1_ref, gc0_ref, gc1_ref,
                pw_ref, ps_ref, wa_ref, wb_ref, wc_ref, wo_ref, n2_ref, wg_ref, wu_ref, wd_ref, o_ref):
    tm = x_ref.shape[0]
    start = (pl.program_id(0) * tm) % seq
    halo = jnp.where(start > 0, halo_ref[...].astype(F32), 0.0)
    yb = _pool_branch(u_ref[...].astype(F32), halo, pw_ref, ps_ref, start)
    gate = lambda lo_ref, hi_ref: _sigmoid(jnp.concatenate([lo_ref[...], hi_ref[...]], axis=1).astype(F32))
    merged = (gate(ga0_ref, ga1_ref) * _dot(ya_ref[...], wa_ref[...])
              + gate(gb0_ref, gb1_ref) * _dot(yb.astype(BF16), wb_ref[...])
              + gate(gc0_ref, gc1_ref) * _dot(yc_ref[...], wc_ref[...]))
    x1 = x_ref[...] + _dot(merged.astype(BF16), wo_ref[...])
    h = _rms(x1, n2_ref[...]).astype(BF16)
    acc = x1
    for c in range(FFN_NCHUNK):
        cols = slice(c * FFN_CHUNK, (c + 1) * FFN_CHUNK)
        g = _dot(h, wg_ref[:, cols])
        u = _dot(h, wu_ref[:, cols])
        act = (g * _sigmoid(g) * u).astype(BF16)
        acc = acc + _dot(act, wd_ref[cols, :])
    o_ref[...] = acc


def _mix(x2d, ya, yc, proj2d, seq, layer, pool_w, pool_scale, wa, wb, wc, wo, norm2_w, wg, wu, wd, tm):
    m = x2d.shape[0]
    c = BRANCH_WIDTH
    assert seq % tm == 0 and tm % POOL_HALO == 0 and POOL_HALO == BF16_SUBLANES
    row = lambda w, col: pl.BlockSpec((tm, w), lambda i: (i, col))
    resident = lambda *shape: pl.BlockSpec((None,) + shape, lambda i: (layer,) + (0,) * len(shape),
                                           pipeline_mode=pl.Buffered(1))
    halo_blocks = tm // POOL_HALO
    gates = [row(c, _OFF[g] // c + half) for g in ("ga", "gb", "gc") for half in range(D_MODEL // c)]
    return pl.pallas_call(
        functools.partial(_mix_kernel, seq),
        out_shape=jax.ShapeDtypeStruct((m, D_MODEL), F32),
        grid=(m // tm,),
        in_specs=[row(D_MODEL, 0), row(c, 0), row(c, 0), row(c, _OFF["pu"] // c),
                  pl.BlockSpec((POOL_HALO, c), lambda i: (jnp.maximum(i * halo_blocks - 1, 0), _OFF["pu"] // c)),
                  *gates,
                  resident(POOL_GROUPS, POOL_GROUP_DIM, POOL_GROUP_DIM), resident(1, c),
                  resident(c, D_MODEL), resident(c, D_MODEL), resident(c, D_MODEL), resident(D_MODEL, D_MODEL),
                  resident(1, D_MODEL), resident(D_MODEL, D_FF), resident(D_MODEL, D_FF), resident(D_FF, D_MODEL)],
        out_specs=row(D_MODEL, 0),
        compiler_params=_params("parallel"),
        name="mix",
    )(x2d, ya, yc, proj2d, proj2d, *([proj2d] * len(gates)), pool_w, pool_scale,
      wa, wb, wc, wo, norm2_w, wg, wu, wd)


def _row_tile(m, want):
    t = min(m, want)
    assert m % t == 0
    return t


def kernel(x, norm1_w, w_in, gla_wg2, gla_bg, gla_norm_w, pool_w, pool_scale, moba_qn_w, moba_kn_w,
           rel_bias, w_up_a, w_up_b, w_up_c, w_out, norm2_w, ffn_w_gate, ffn_w_up, ffn_w_down):
    b, s, d = x.shape
    assert d == D_MODEL and s % MOBA_BLOCK == 0 and s % GLA_T == 0
    m = b * s
    bf = lambda a: a.astype(BF16)
    bias_tab = _bias_tables(rel_bias)
    w_in_bf = bf(w_in)
    norm1 = norm1_w.reshape(DEPTH, 1, d)
    qn = moba_qn_w.reshape(DEPTH, 1, MOBA_HEAD_DIM)
    kn = moba_kn_w.reshape(DEPTH, 1, MOBA_HEAD_DIM)
    mix_weights = (bf(pool_w), pool_scale.reshape(DEPTH, 1, -1), bf(w_up_a), bf(w_up_b), bf(w_up_c), bf(w_out),
                   norm2_w.reshape(DEPTH, 1, d), bf(ffn_w_gate), bf(ffn_w_up), bf(ffn_w_down))
    x2d = x.reshape(m, d)
    for l in range(DEPTH):
        proj2d, kmean = _inproj(x2d, norm1, w_in_bf, qn, kn, l, _row_tile(m, 512))
        proj = proj2d.reshape(b, s, PROJ_COLS)
        ya = _gla(proj, gla_wg2[l], gla_bg[l], gla_norm_w[l])
        yc = _moba(proj, kmean, rel_bias, bias_tab)
        x2d = _mix(x2d, ya.reshape(m, -1), yc.reshape(m, -1), proj2d, s, l, *mix_weights, _row_tile(s, 512))
    return x2d.reshape(b, s, d)
```

```python
import functools
import math

import numpy as np
import jax
import jax.numpy as jnp
from jax import lax
from jax.experimental import pallas as pl
from jax.experimental.pallas import tpu as pltpu

D_MODEL = 1024
DEPTH = 2
BRANCH_WIDTH = D_MODEL // 2
GLA_HEADS = 4
GLA_DV = BRANCH_WIDTH // GLA_HEADS
GLA_DK = GLA_DV // 2
GLA_LOW_RANK = 16
GLA_GATE_NORMALIZER = 16.0
POOL_WINDOWS = (2, 4, 8, 16)
POOL_GROUPS = 4
POOL_GROUP_DIM = BRANCH_WIDTH // POOL_GROUPS
MOBA_HEADS = 4
MOBA_HEAD_DIM = BRANCH_WIDTH // MOBA_HEADS
MOBA_BLOCK = 256
MOBA_TOPK = 3
REL_BUCKETS = 32
REL_MAX_EXACT = REL_BUCKETS // 2
REL_MAX_DIST = 128
D_FF = ((8 * D_MODEL + 767) // 768) * 256
NORM_EPS = 1e-6
NEG_INF = -1e30
LOG2E = math.log2(math.e)

LANES = 128
BF16_SUBLANES = 16
VMEM_LIMIT_BYTES = 56 * 1024 * 1024

_SECTIONS = (("gq", 256), ("gk", 256), ("gv", 512), ("glr", GLA_LOW_RANK), ("gr", 512), ("pu", 512),
             ("mq", 512), ("mk", 512), ("mv", 512), ("ga", 1024), ("gb", 1024), ("gc", 1024))
_OFF = {}
_o = 0
for _n, _w in _SECTIONS:
    if _n != "glr":
        assert _o % min(_w, 512) == 0
        _OFF[_n] = _o
        _o += _w
_OFF["glr"] = _o
PROJ_COLS = _o + LANES
GLR_SRC = sum(w for n, w in _SECTIONS[:3])
PROJ_CHUNK = 512

GLA_T = 128
GLA_LEVELS = 7
GLA_BATCH_PER_STEP = 4
MOBA_BATCH_PER_STEP = 4
POOL_HALO = 16
FFN_CHUNK = 256

F32 = jnp.float32
BF16 = jnp.bfloat16


def _params(*sem):
    return pltpu.CompilerParams(dimension_semantics=sem, vmem_limit_bytes=VMEM_LIMIT_BYTES)


def _dot(a, b):
    return jnp.dot(a, b, preferred_element_type=F32)


def _dot_nt(a, b):
    return lax.dot_general(a, b, (((1,), (1,)), ((), ())), preferred_element_type=F32)


def _split_hi_lo(x):
    hi = x.astype(BF16)
    lo = (x - hi.astype(F32)).astype(BF16)
    return hi, lo


def _rms(x, w):
    return x * lax.rsqrt(jnp.mean(x * x, axis=-1, keepdims=True) + NORM_EPS) * w


def _sigmoid(x):
    return 1.0 / (1.0 + jnp.exp(-x))


def _head_rms(y, w, scale=1.0):
    dh = MOBA_HEAD_DIM
    return jnp.concatenate([_rms(y[:, h * dh:(h + 1) * dh], w) * scale for h in range(y.shape[1] // dh)], axis=1)


def _inproj_kernel(x_ref, nw_ref, w_ref, qn_ref, kn_ref, o_ref, km_ref, wp_ref):
    @pl.when(pl.program_id(0) == 0)
    def _():
        glr_end = GLR_SRC + GLA_LOW_RANK
        wp_ref[:, :GLR_SRC] = w_ref[:, :GLR_SRC]
        wp_ref[:, GLR_SRC:_OFF["glr"]] = w_ref[:, glr_end:]
        wp_ref[:, _OFF["glr"]:] = jnp.concatenate(
            [w_ref[:, GLR_SRC:glr_end], jnp.zeros((D_MODEL, LANES - GLA_LOW_RANK), BF16)], axis=1)

    h = _rms(x_ref[...], nw_ref[...]).astype(BF16)
    for lo in range(0, PROJ_COLS, PROJ_CHUNK):
        cols = slice(lo, min(lo + PROJ_CHUNK, PROJ_COLS))
        y = _dot(h, wp_ref[:, cols])
        if lo == _OFF["gr"]:
            y = y * _sigmoid(y)
        elif lo == _OFF["mq"]:
            y = _head_rms(y, qn_ref[...], MOBA_HEAD_DIM ** -0.5 * LOG2E)
        elif lo == _OFF["mk"]:
            y = _head_rms(y, kn_ref[...])
            for j in range(y.shape[0] // MOBA_BLOCK):
                km_ref[j] = jnp.mean(y[j * MOBA_BLOCK:(j + 1) * MOBA_BLOCK], axis=0, keepdims=True)
        o_ref[:, cols] = y.astype(BF16)


def _inproj(x2d, norm_w, w_bf, qn_w, kn_w, layer, tm):
    m = x2d.shape[0]
    in_cols = w_bf.shape[-1]
    assert tm % MOBA_BLOCK == 0 and all(_OFF[n] % PROJ_CHUNK == 0 for n in ("gr", "mq", "mk"))
    layer_block = lambda *shape: pl.BlockSpec((None,) + shape, lambda i: (layer,) + (0,) * len(shape))
    return pl.pallas_call(
        _inproj_kernel,
        out_shape=(jax.ShapeDtypeStruct((m, PROJ_COLS), BF16),
                   jax.ShapeDtypeStruct((m // MOBA_BLOCK, 1, BRANCH_WIDTH), F32)),
        grid=(m // tm,),
        in_specs=[pl.BlockSpec((tm, D_MODEL), lambda i: (i, 0)),
                  layer_block(1, D_MODEL),
                  pl.BlockSpec((None, D_MODEL, in_cols), lambda i: (layer, 0, 0), pipeline_mode=pl.Buffered(1)),
                  layer_block(1, MOBA_HEAD_DIM), layer_block(1, MOBA_HEAD_DIM)],
        out_specs=(pl.BlockSpec((tm, PROJ_COLS), lambda i: (i, 0)),
                   pl.BlockSpec((tm // MOBA_BLOCK, 1, BRANCH_WIDTH), lambda i: (i, 0, 0))),
        scratch_shapes=[pltpu.VMEM((D_MODEL, PROJ_COLS), BF16)],
        compiler_params=_params("arbitrary"),
        name="inproj",
    )(x2d, norm_w, w_bf, qn_w, kn_w)


def _gla_constants():
    t = np.arange(GLA_T)
    u = np.arange(GLA_T)
    a_rows = []
    level = np.full((GLA_T, GLA_T), -1, np.int32)
    for lev in range(GLA_LEVELS):
        s = (GLA_T // 2) >> lev
        g = 2 * s
        first = (t % g) < s
        ref = (t // g) * g + s - 1
        a = np.where(first[:, None],
                     (u[None, :] > t[:, None]) & (u[None, :] <= ref[:, None]),
                     (u[None, :] > ref[:, None]) & (u[None, :] <= t[:, None]))
        a_rows.append(a)
        same = (t[:, None] // g) == (t[None, :] // g)
        level[same & (~first)[:, None] & first[None, :]] = lev
    a_rows.append(u[None, :] <= t[:, None])
    level[np.eye(GLA_T, dtype=bool)] = GLA_LEVELS
    a_all = np.concatenate(a_rows, axis=0).astype(np.float32)
    akt = np.concatenate([(u[:, None] > t[None, :]), np.ones((GLA_T, GLA_T), bool)], axis=1).astype(np.float32)
    return np.concatenate([a_all, a_all], axis=1), np.concatenate([akt, akt], axis=0), level


def _log_sigmoid(z):
    return jnp.minimum(z, 0.0) - jnp.log1p(jnp.exp(-jnp.abs(z)))


def _gla_kernel(q_ref, k_ref, v_ref, sr_ref, glr_ref, wg2_ref, bg_ref, a_ref, akt_ref, lvl_ref, nw_ref, o_ref,
                s_ref, sbd_ref):
    nh, dk, dv, t = GLA_HEADS, GLA_DK, GLA_DV, GLA_T

    @pl.when(pl.program_id(1) == 0)
    def _():
        s_ref[...] = jnp.zeros_like(s_ref)
        sbd_ref[...] = jnp.zeros_like(sbd_ref)

    rows = range(q_ref.shape[0])
    each = lambda f: [f(bi) for bi in rows]
    q8 = each(lambda bi: q_ref[bi] * (dk ** -0.5))
    k = each(lambda bi: k_ref[bi])
    v = each(lambda bi: v_ref[bi])
    s_old = each(lambda bi: [s_ref[bi, h] for h in range(nh)])

    inv = 1.0 / GLA_GATE_NORMALIZER
    glog = each(lambda bi: _log_sigmoid(_dot(glr_ref[bi], wg2_ref[...]) + bg_ref[...]) * inv)
    glog_t = each(lambda bi: glog[bi].T)

    dec_all = each(lambda bi: jnp.exp(_dot(a_ref[...], jnp.concatenate(_split_hi_lo(glog[bi]), axis=0))).astype(BF16))
    et_all = each(lambda bi: _dot(jnp.concatenate(_split_hi_lo(glog_t[bi]), axis=1), akt_ref[...]))

    qe = each(lambda bi: q8[bi] * dec_all[bi][GLA_LEVELS * t:(GLA_LEVELS + 1) * t])
    groups = range(nh * dk // LANES)
    o_inter = each(lambda bi: jnp.concatenate(
        [_dot(qe[bi][:, g * LANES:(g + 1) * LANES], sbd_ref[bi, g]) for g in groups], axis=1))

    per_group = LANES // dk
    lane_head = lax.broadcasted_iota(jnp.int32, (t, LANES), 1) // dk
    head_masks = [lane_head == h for h in range(per_group)]
    lvl = lvl_ref[...]
    attn = each(lambda bi: [jnp.zeros((t, t), F32) for _ in range(nh)])
    for lev in range(GLA_LEVELS + 1):
        m = lvl == lev
        for bi in rows:
            if lev < GLA_LEVELS:
                e = dec_all[bi][lev * t:(lev + 1) * t]
                ql = q8[bi] * e
                kl = k[bi] * e
            else:
                ql = q8[bi]
                kl = k[bi]
            for g in range(nh // per_group):
                lanes = slice(g * LANES, (g + 1) * LANES)
                klg = kl[:, lanes]
                kbd = jnp.concatenate([jnp.where(hm, klg, jnp.zeros_like(klg)) for hm in head_masks], axis=0)
                sc = _dot_nt(ql[:, lanes], kbd)
                for j in range(per_group):
                    h = g * per_group + j
                    attn[bi][h] = jnp.where(m, sc[:, j * t:(j + 1) * t], attn[bi][h])

    nw = nw_ref[...]
    for bi in rows:
        outs = []
        for h in range(nh):
            cols = slice(h * dv, (h + 1) * dv)
            outs.append(_rms(_dot(attn[bi][h].astype(BF16), v[bi][:, cols]) + o_inter[bi][:, cols], nw))
        o_ref[bi] = (jnp.concatenate(outs, axis=1) * sr_ref[bi].astype(F32)).astype(BF16)

    k_tail_t = each(lambda bi: (k[bi].astype(F32).T * jnp.exp(et_all[bi][:, :t])).astype(BF16))
    dec_t = each(lambda bi: jnp.exp(et_all[bi][:, t:]))
    for bi in rows:
        for h in range(nh):
            ch = slice(h * dk, (h + 1) * dk)
            s_new = dec_t[bi][ch, :] * s_old[bi][h] + _dot(k_tail_t[bi][ch, :], v[bi][:, h * dv:(h + 1) * dv])
            s_ref[bi, h] = s_new
            j = h % (LANES // dk)
            sbd_ref[bi, h * dk // LANES, j * dk:(j + 1) * dk, j * dv:(j + 1) * dv] = s_new.astype(BF16)


def _gla(proj, wg2, bg, norm_w):
    b, s, _ = proj.shape
    t = GLA_T
    hk = GLA_HEADS * GLA_DK
    hv = GLA_HEADS * GLA_DV
    a_all, akt, level = _gla_constants()
    wg2_p = jnp.zeros((LANES, hk), F32).at[:GLA_LOW_RANK].set(wg2).astype(BF16)
    const = lambda *shape: pl.BlockSpec(shape, lambda i, j: (0,) * len(shape))
    nb = GLA_BATCH_PER_STEP if b % GLA_BATCH_PER_STEP == 0 else 1
    return pl.pallas_call(
        _gla_kernel,
        out_shape=jax.ShapeDtypeStruct((b, s, hv), BF16),
        grid=(b // nb, s // t),
        in_specs=[pl.BlockSpec((nb, t, hk), lambda i, j: (i, j, _OFF["gq"] // hk)),
                  pl.BlockSpec((nb, t, hk), lambda i, j: (i, j, _OFF["gk"] // hk)),
                  pl.BlockSpec((nb, t, hv), lambda i, j: (i, j, _OFF["gv"] // hv)),
                  pl.BlockSpec((nb, t, hv), lambda i, j: (i, j, _OFF["gr"] // hv)),
                  pl.BlockSpec((nb, t, LANES), lambda i, j: (i, j, _OFF["glr"] // LANES)),
                  const(LANES, hk), const(1, hk),
                  const((GLA_LEVELS + 1) * t, 2 * t), const(2 * t, 2 * t), const(t, t),
                  const(1, GLA_DV)],
        out_specs=pl.BlockSpec((nb, t, hv), lambda i, j: (i, j, 0)),
        scratch_shapes=[pltpu.VMEM((nb, GLA_HEADS, GLA_DK, GLA_DV), F32),
                        pltpu.VMEM((nb, hk // LANES, LANES, LANES // GLA_DK * GLA_DV), BF16)],
        compiler_params=_params("parallel", "arbitrary"),
        name="gla",
    )(proj, proj, proj, proj, proj, wg2_p, bg.reshape(1, hk),
      jnp.asarray(a_all, BF16), jnp.asarray(akt, BF16), jnp.asarray(level), norm_w.reshape(1, GLA_DV))


def _t5_bucket_np(dist):
    n = np.maximum(dist, 0)
    large = REL_MAX_EXACT + (np.log(np.maximum(n, 1).astype(np.float32) / REL_MAX_EXACT)
                             / math.log(REL_MAX_DIST / REL_MAX_EXACT)
                             * (REL_BUCKETS - REL_MAX_EXACT)).astype(np.int32)
    large = np.minimum(large, REL_BUCKETS - 1)
    return np.where(n < REL_MAX_EXACT, n, large).astype(np.int32)


def _bias_kernel(rb_ref, bk_ref, o_ref):
    h = pl.program_id(0)
    for which in range(2):
        bk = bk_ref[which]
        acc = jnp.full(bk.shape, NEG_INF, F32)
        for bucket in range(REL_BUCKETS):
            acc = jnp.where(bk == bucket, rb_ref[bucket, h] * LOG2E, acc)
        o_ref[0, which] = acc


def _bias_tables(rel_bias):
    blk = MOBA_BLOCK
    j = np.arange(blk)[:, None]
    i = np.arange(blk)[None, :]
    own = np.where(i >= j, _t5_bucket_np(i - j), -1)
    prev = _t5_bucket_np(i + blk - j)
    buckets = np.stack([own, prev]).astype(np.int32)
    return pl.pallas_call(
        _bias_kernel,
        out_shape=jax.ShapeDtypeStruct((MOBA_HEADS, 2, blk, blk), F32),
        grid=(MOBA_HEADS,),
        in_specs=[pl.BlockSpec(memory_space=pltpu.SMEM),
                  pl.BlockSpec((2, blk, blk), lambda h: (0, 0, 0))],
        out_specs=pl.BlockSpec((1, 2, blk, blk), lambda h: (h, 0, 0, 0)),
        compiler_params=_params("parallel"),
        name="moba_bias",
    )(rel_bias, jnp.asarray(buckets))


def _moba_kernel(rb_ref, q_ref, k_ref, v_ref, km_ref, bias_ref, o_ref, vt_ref, sel_ref):
    blk, dh, nh = MOBA_BLOCK, MOBA_HEAD_DIM, MOBA_HEADS
    nb = k_ref.shape[1] // blk
    qb = pl.program_id(1)
    chains = [(bi, h) for bi in range(q_ref.shape[0]) for h in range(nh)]
    each = lambda f: [f(c, bi, h) for c, (bi, h) in enumerate(chains)]
    head = lambda h: slice(h * dh, (h + 1) * dh)
    keys = lambda bi, h, n: k_ref[bi, pl.ds(pl.multiple_of(n * blk, blk), blk), head(h)]

    @pl.when(qb == 0)
    def _():
        ones = jnp.ones((BF16_SUBLANES, blk), BF16)
        for c, (bi, h) in enumerate(chains):
            for n in range(nb):
                vt = v_ref[bi, n * blk:(n + 1) * blk, head(h)].astype(F32).T.astype(BF16)
                vt_ref[c, n] = jnp.concatenate([vt, ones], axis=0)

    n_iota = lax.broadcasted_iota(jnp.int32, (nb, blk), 0)
    q_sc = each(lambda c, bi, h: q_ref[bi, :, head(h)])

    s_own = each(lambda c, bi, h: _dot_nt(keys(bi, h, qb), q_sc[c]) + bias_ref[h, 0])
    m0 = each(lambda c, bi, h: jnp.max(s_own[c], axis=0, keepdims=True))
    p0 = each(lambda c, bi, h: jnp.exp2(s_own[c] - m0[c]))
    carry = each(lambda c, bi, h: (m0[c], _dot(vt_ref[c, qb], p0[c].astype(BF16))))

    for c, (bi, h) in enumerate(chains):
        km_hi, km_lo = _split_hi_lo(km_ref[bi * nb:(bi + 1) * nb, 0, head(h)])
        sc = _dot_nt(km_hi, q_sc[c]) + _dot_nt(km_lo, q_sc[c])
        rank = jnp.zeros((nb, blk), jnp.int32)
        for m in range(nb):
            sm = sc[m:m + 1, :]
            beats = ((sm > sc) | ((sm == sc) & (m < n_iota))) & (m < qb)
            rank = rank + beats.astype(jnp.int32)
        sel_ref[c] = jnp.where((n_iota < qb) & (rank < MOBA_TOPK), 0.0, NEG_INF)

    def attend(n, tile_bias, row_bias, carry):
        s = each(lambda c, bi, h: _dot_nt(keys(bi, h, n), q_sc[c]))
        if tile_bias is not None:
            s = each(lambda c, bi, h: s[c] + tile_bias(c, h))
        rowb = each(lambda c, bi, h: row_bias(c, h))
        m_new = each(lambda c, bi, h: jnp.maximum(carry[c][0], jnp.max(s[c], axis=0, keepdims=True) + rowb[c]))
        alpha = each(lambda c, bi, h: jnp.exp2(carry[c][0] - m_new[c]))
        p = each(lambda c, bi, h: jnp.exp2(s[c] - (m_new[c] - rowb[c])))
        pv = each(lambda c, bi, h: _dot(vt_ref[c, n], p[c].astype(BF16)))
        return each(lambda c, bi, h: (m_new[c], alpha[c] * carry[c][1] + pv[c]))

    prev = jnp.maximum(qb - 1, 0)
    carry = attend(prev, lambda c, h: bias_ref[h, 1], lambda c, h: sel_ref[c, pl.ds(prev, 1), :], carry)
    carry = lax.fori_loop(
        0, prev,
        lambda n, cr: attend(n, None, lambda c, h: sel_ref[c, pl.ds(n, 1), :] + rb_ref[REL_BUCKETS - 1, h] * LOG2E,
                             cr), carry)
    outs = each(lambda c, bi, h: (carry[c][1][:dh] / carry[c][1][dh:dh + 1]).T)
    for bi in range(q_ref.shape[0]):
        o_ref[bi] = jnp.concatenate(outs[bi * nh:(bi + 1) * nh], axis=1).astype(BF16)


def _moba(proj, kmean, rel_bias, bias_tab):
    b, s, _ = proj.shape
    blk, dh, nh = MOBA_BLOCK, MOBA_HEAD_DIM, MOBA_HEADS
    c = nh * dh
    nb = s // blk
    rows = MOBA_BATCH_PER_STEP if b % MOBA_BATCH_PER_STEP == 0 else 1
    nc = rows * nh
    return pl.pallas_call(
        _moba_kernel,
        out_shape=jax.ShapeDtypeStruct((b, s, c), BF16),
        grid=(b // rows, nb),
        in_specs=[pl.BlockSpec(memory_space=pltpu.SMEM),
                  pl.BlockSpec((rows, blk, c), lambda i, j: (i, j, _OFF["mq"] // c)),
                  pl.BlockSpec((rows, s, c), lambda i, j: (i, 0, _OFF["mk"] // c), pipeline_mode=pl.Buffered(1)),
                  pl.BlockSpec((rows, s, c), lambda i, j: (i, 0, _OFF["mv"] // c), pipeline_mode=pl.Buffered(1)),
                  pl.BlockSpec((rows * nb, 1, c), lambda i, j: (i, 0, 0)),
                  pl.BlockSpec((nh, 2, blk, blk), lambda i, j: (0, 0, 0, 0))],
        out_specs=pl.BlockSpec((rows, blk, c), lambda i, j: (i, j, 0)),
        scratch_shapes=[pltpu.VMEM((nc, nb, dh + BF16_SUBLANES, blk), BF16), pltpu.VMEM((nc, nb, blk), F32)],
        compiler_params=_params("parallel", "arbitrary"),
        name="moba",
    )(rel_bias, proj, proj, proj, kmean, bias_tab)


def _pool_branch(cur, halo, pw_ref, ps_ref, start):
    tm = cur.shape[0]
    ext = jnp.concatenate([halo, cur], axis=0)
    pos = start + lax.broadcasted_iota(jnp.int32, (tm, 1), 0)
    outs = []
    for g, w in enumerate(POOL_WINDOWS):
        cols = slice(g * POOL_GROUP_DIM, (g + 1) * POOL_GROUP_DIM)
        a = ext[:, cols]
        shift = 1
        while shift < w:
            a = a + pltpu.roll(a, shift, 0)
            shift *= 2
        cnt = jnp.minimum(pos + 1, w).astype(F32)
        p = a[POOL_HALO:, :] / cnt - cur[:, cols]
        outs.append(_dot(p.astype(BF16), pw_ref[g]))
    return jnp.concatenate(outs, axis=1) * ps_ref[...]


def _mix_kernel(seq, x_ref, ya_ref, yc_ref, u_ref, halo_ref, ga0_ref, ga1_ref, gb0_ref, gb1_ref, gc0_ref, gc1_ref,
                pw_ref, ps_ref, wa_ref, wb_ref, wc_ref, wo_ref, n2_ref, wg_ref, wu_ref, wd_ref, o_ref):
    tm = x_ref.shape[0]
    start = (pl.program_id(0) * tm) % seq
    halo = jnp.where(start > 0, halo_ref[...].astype(F32), 0.0)
    yb = _pool_branch(u_ref[...].astype(F32), halo, pw_ref, ps_ref, start)
    gate = lambda lo_ref, hi_ref: _sigmoid(jnp.concatenate([lo_ref[...], hi_ref[...]], axis=1).astype(F32))
    merged = (gate(ga0_ref, ga1_ref) * _dot(ya_ref[...], wa_ref[...])
              + gate(gb0_ref, gb1_ref) * _dot(yb.astype(BF16), wb_ref[...])
              + gate(gc0_ref, gc1_ref) * _dot(yc_ref[...], wc_ref[...]))
    x1 = x_ref[...] + _dot(merged.astype(BF16), wo_ref[...])
    h = _rms(x1, n2_ref[...]).astype(BF16)
    acc = x1
    for lo in range(0, D_FF, FFN_CHUNK):
        cols = slice(lo, min(lo + FFN_CHUNK, D_FF))
        g = _dot(h, wg_ref[:, cols])
        u = _dot(h, wu_ref[:, cols])
        act = (g * _sigmoid(g) * u).astype(BF16)
        acc = acc + _dot(act, wd_ref[cols, :])
    o_ref[...] = acc


def _mix(x2d, ya, yc, proj2d, seq, layer, pool_w, pool_scale, wa, wb, wc, wo, norm2_w, wg, wu, wd, tm):
    m = x2d.shape[0]
    c = BRANCH_WIDTH
    assert seq % tm == 0 and tm % POOL_HALO == 0 and POOL_HALO == BF16_SUBLANES
    row = lambda w, col: pl.BlockSpec((tm, w), lambda i: (i, col))
    resident = lambda *shape: pl.BlockSpec((None,) + shape, lambda i: (layer,) + (0,) * len(shape),
                                           pipeline_mode=pl.Buffered(1))
    halo_blocks = tm // POOL_HALO
    gates = [row(c, _OFF[g] // c + half) for g in ("ga", "gb", "gc") for half in range(D_MODEL // c)]
    return pl.pallas_call(
        functools.partial(_mix_kernel, seq),
        out_shape=jax.ShapeDtypeStruct((m, D_MODEL), F32),
        grid=(m // tm,),
        in_specs=[row(D_MODEL, 0), row(c, 0), row(c, 0), row(c, _OFF["pu"] // c),
                  pl.BlockSpec((POOL_HALO, c), lambda i: (jnp.maximum(i * halo_blocks - 1, 0), _OFF["pu"] // c)),
                  *gates,
                  resident(POOL_GROUPS, POOL_GROUP_DIM, POOL_GROUP_DIM), resident(1, c),
                  resident(c, D_MODEL), resident(c, D_MODEL), resident(c, D_MODEL), resident(D_MODEL, D_MODEL),
                  resident(1, D_MODEL), resident(D_MODEL, D_FF), resident(D_MODEL, D_FF), resident(D_FF, D_MODEL)],
        out_specs=row(D_MODEL, 0),
        compiler_params=_params("parallel"),
        name="mix",
    )(x2d, ya, yc, proj2d, proj2d, *([proj2d] * len(gates)), pool_w, pool_scale,
      wa, wb, wc, wo, norm2_w, wg, wu, wd)


def _row_tile(m, want):
    t = min(m, want)
    assert m % t == 0
    return t


def kernel(x, norm1_w, w_in, gla_wg2, gla_bg, gla_norm_w, pool_w, pool_scale, moba_qn_w, moba_kn_w,
           rel_bias, w_up_a, w_up_b, w_up_c, w_out, norm2_w, ffn_w_gate, ffn_w_up, ffn_w_down):
    b, s, d = x.shape
    assert d == D_MODEL and s % MOBA_BLOCK == 0 and s % GLA_T == 0
    m = b * s
    bf = lambda a: a.astype(BF16)
    bias_tab = _bias_tables(rel_bias)
    w_in_bf = bf(w_in)
    norm1 = norm1_w.reshape(DEPTH, 1, d)
    qn = moba_qn_w.reshape(DEPTH, 1, MOBA_HEAD_DIM)
    kn = moba_kn_w.reshape(DEPTH, 1, MOBA_HEAD_DIM)
    mix_weights = (bf(pool_w), pool_scale.reshape(DEPTH, 1, -1), bf(w_up_a), bf(w_up_b), bf(w_up_c), bf(w_out),
                   norm2_w.reshape(DEPTH, 1, d), bf(ffn_w_gate), bf(ffn_w_up), bf(ffn_w_down))
    x2d = x.reshape(m, d)
    for l in range(DEPTH):
        proj2d, kmean = _inproj(x2d, norm1, w_in_bf, qn, kn, l, _row_tile(m, 512))
        proj = proj2d.reshape(b, s, PROJ_COLS)
        ya = _gla(proj, gla_wg2[l], gla_bg[l], gla_norm_w[l])
        yc = _moba(proj, kmean, rel_bias, bias_tab)
        x2d = _mix(x2d, ya.reshape(m, -1), yc.reshape(m, -1), proj2d, s, l, *mix_weights, _row_tile(s, 512))
    return x2d.reshape(b, s, d)
```

```python
import functools
import math

import numpy as np
import jax
import jax.numpy as jnp
from jax import lax
from jax.experimental import pallas as pl
from jax.experimental.pallas import tpu as pltpu

D_MODEL = 1024
DEPTH = 2
BRANCH_WIDTH = D_MODEL // 2
GLA_HEADS = 4
GLA_DV = BRANCH_WIDTH // GLA_HEADS
GLA_DK = GLA_DV // 2
GLA_LOW_RANK = 16
GLA_GATE_NORMALIZER = 16.0
POOL_WINDOWS = (2, 4, 8, 16)
POOL_GROUPS = 4
POOL_GROUP_DIM = BRANCH_WIDTH // POOL_GROUPS
MOBA_HEADS = 4
MOBA_HEAD_DIM = BRANCH_WIDTH // MOBA_HEADS
MOBA_BLOCK = 256
MOBA_TOPK = 3
REL_BUCKETS = 32
REL_MAX_EXACT = REL_BUCKETS // 2
REL_MAX_DIST = 128
D_FF = ((8 * D_MODEL + 767) // 768) * 256
NORM_EPS = 1e-6
NEG_INF = -1e30
LOG2E = math.log2(math.e)

LANES = 128
BF16_SUBLANES = 16
VMEM_LIMIT_BYTES = 56 * 1024 * 1024

_SECTIONS = (("gq", 256), ("gk", 256), ("gv", 512), ("glr", GLA_LOW_RANK), ("gr", 512), ("pu", 512),
             ("mq", 512), ("mk", 512), ("mv", 512), ("ga", 1024), ("gb", 1024), ("gc", 1024))
_OFF = {}
_o = 0
for _n, _w in _SECTIONS:
    if _n != "glr":
        assert _o % min(_w, 512) == 0
        _OFF[_n] = _o
        _o += _w
_OFF["glr"] = _o
PROJ_COLS = _o + LANES
GLR_SRC = sum(w for n, w in _SECTIONS[:3])
PROJ_CHUNK = 512
W_STAGE_ROWS = 128

GLA_T = 128
GLA_LEVELS = 7
GLA_BATCH_PER_STEP = 4
MOBA_BATCH_PER_STEP = 4
POOL_HALO = 16
FFN_CHUNK = 256

F32 = jnp.float32
BF16 = jnp.bfloat16


def _params(*sem):
    return pltpu.CompilerParams(dimension_semantics=sem, vmem_limit_bytes=VMEM_LIMIT_BYTES)


def _dot(a, b):
    return jnp.dot(a, b, preferred_element_type=F32)


def _dot_nt(a, b):
    return lax.dot_general(a, b, (((1,), (1,)), ((), ())), preferred_element_type=F32)


def _split_hi_lo(x):
    hi = x.astype(BF16)
    lo = (x - hi.astype(F32)).astype(BF16)
    return hi, lo


def _rms(x, w):
    return x * lax.rsqrt(jnp.mean(x * x, axis=-1, keepdims=True) + NORM_EPS) * w


def _sigmoid(x):
    return 1.0 / (1.0 + jnp.exp(-x))


def _head_rms(y, w, scale=1.0):
    dh = MOBA_HEAD_DIM
    return jnp.concatenate([_rms(y[:, h * dh:(h + 1) * dh], w) * scale for h in range(y.shape[1] // dh)], axis=1)


def _stage_weights(layer, plan, stage_ref, sem):
    rows = stage_ref.shape[1]
    chunks = [(src, store, r0) for src, store in plan for r0 in range(0, src.shape[1], rows)]

    def chunk_copy(i):
        src, _, r0 = chunks[i]
        return pltpu.make_async_copy(src.at[layer, pl.ds(r0, rows), :],
                                     stage_ref.at[i % 2, :, pl.ds(0, src.shape[2])], sem.at[i % 2])

    chunk_copy(0).start()
    for i, (src, store, r0) in enumerate(chunks):
        if i + 1 < len(chunks):
            chunk_copy(i + 1).start()
        chunk_copy(i).wait()
        store(r0, stage_ref[i % 2, :, :src.shape[2]].astype(BF16))


def _inproj_kernel(layer, x_ref, nw_ref, w_hbm, qn_ref, kn_ref, o_ref, km_ref, wp_ref, stage_ref, sem):
    @pl.when(pl.program_id(0) == 0)
    def _():
        glr_end = GLR_SRC + GLA_LOW_RANK

        def store(r0, w):
            dst = slice(r0, r0 + w.shape[0])
            wp_ref[dst, :GLR_SRC] = w[:, :GLR_SRC]
            wp_ref[dst, GLR_SRC:_OFF["glr"]] = w[:, glr_end:]
            wp_ref[dst, _OFF["glr"]:] = jnp.concatenate(
                [w[:, GLR_SRC:glr_end], jnp.zeros((w.shape[0], LANES - GLA_LOW_RANK), BF16)], axis=1)

        _stage_weights(layer, [(w_hbm, store)], stage_ref, sem)

    h = _rms(x_ref[...], nw_ref[...]).astype(BF16)
    for lo in range(0, PROJ_COLS, PROJ_CHUNK):
        cols = slice(lo, min(lo + PROJ_CHUNK, PROJ_COLS))
        y = _dot(h, wp_ref[:, cols])
        if lo == _OFF["gr"]:
            y = y * _sigmoid(y)
        elif lo == _OFF["mq"]:
            y = _head_rms(y, qn_ref[...], MOBA_HEAD_DIM ** -0.5 * LOG2E)
        elif lo == _OFF["mk"]:
            y = _head_rms(y, kn_ref[...])
            for j in range(y.shape[0] // MOBA_BLOCK):
                km_ref[j] = jnp.mean(y[j * MOBA_BLOCK:(j + 1) * MOBA_BLOCK], axis=0, keepdims=True)
        o_ref[:, cols] = y.astype(BF16)


def _inproj(x2d, norm_w, w_in, qn_w, kn_w, layer, tm):
    m = x2d.shape[0]
    in_cols = w_in.shape[-1]
    assert tm % MOBA_BLOCK == 0 and all(_OFF[n] % PROJ_CHUNK == 0 for n in ("gr", "mq", "mk"))
    assert D_MODEL % W_STAGE_ROWS == 0
    layer_block = lambda *shape: pl.BlockSpec((None,) + shape, lambda i: (layer,) + (0,) * len(shape))
    return pl.pallas_call(
        functools.partial(_inproj_kernel, layer),
        out_shape=(jax.ShapeDtypeStruct((m, PROJ_COLS), BF16),
                   jax.ShapeDtypeStruct((m // MOBA_BLOCK, 1, BRANCH_WIDTH), F32)),
        grid=(m // tm,),
        in_specs=[pl.BlockSpec((tm, D_MODEL), lambda i: (i, 0)),
                  layer_block(1, D_MODEL),
                  pl.BlockSpec(memory_space=pl.ANY),
                  layer_block(1, MOBA_HEAD_DIM), layer_block(1, MOBA_HEAD_DIM)],
        out_specs=(pl.BlockSpec((tm, PROJ_COLS), lambda i: (i, 0)),
                   pl.BlockSpec((tm // MOBA_BLOCK, 1, BRANCH_WIDTH), lambda i: (i, 0, 0))),
        scratch_shapes=[pltpu.VMEM((D_MODEL, PROJ_COLS), BF16), pltpu.VMEM((2, W_STAGE_ROWS, in_cols), F32),
                        pltpu.SemaphoreType.DMA((2,))],
        compiler_params=_params("arbitrary"),
        name="inproj",
    )(x2d, norm_w, w_in, qn_w, kn_w)


def _gla_constants():
    t = np.arange(GLA_T)
    u = np.arange(GLA_T)
    a_rows = []
    level = np.full((GLA_T, GLA_T), -1, np.int32)
    for lev in range(GLA_LEVELS):
        s = (GLA_T // 2) >> lev
        g = 2 * s
        first = (t % g) < s
        ref = (t // g) * g + s - 1
        a = np.where(first[:, None],
                     (u[None, :] > t[:, None]) & (u[None, :] <= ref[:, None]),
                     (u[None, :] > ref[:, None]) & (u[None, :] <= t[:, None]))
        a_rows.append(a)
        same = (t[:, None] // g) == (t[None, :] // g)
        level[same & (~first)[:, None] & first[None, :]] = lev
    a_rows.append(u[None, :] <= t[:, None])
    level[np.eye(GLA_T, dtype=bool)] = GLA_LEVELS
    a_all = np.concatenate(a_rows, axis=0).astype(np.float32)
    akt = np.concatenate([(u[:, None] > t[None, :]), np.ones((GLA_T, GLA_T), bool)], axis=1).astype(np.float32)
    return np.concatenate([a_all, a_all], axis=1), np.concatenate([akt, akt], axis=0), level


def _log_sigmoid(z):
    return jnp.minimum(z, 0.0) - jnp.log1p(jnp.exp(-jnp.abs(z)))


def _gla_kernel(q_ref, k_ref, v_ref, sr_ref, glr_ref, wg2_ref, bg_ref, a_ref, akt_ref, lvl_ref, nw_ref, o_ref,
                s_ref, sbd_ref):
    nh, dk, dv, t = GLA_HEADS, GLA_DK, GLA_DV, GLA_T

    @pl.when(pl.program_id(1) == 0)
    def _():
        s_ref[...] = jnp.zeros_like(s_ref)
        sbd_ref[...] = jnp.zeros_like(sbd_ref)

    rows = range(q_ref.shape[0])
    each = lambda f: [f(bi) for bi in rows]
    q8 = each(lambda bi: q_ref[bi] * (dk ** -0.5))
    k = each(lambda bi: k_ref[bi])
    v = each(lambda bi: v_ref[bi])
    s_old = each(lambda bi: [s_ref[bi, h] for h in range(nh)])

    inv = 1.0 / GLA_GATE_NORMALIZER
    glog = each(lambda bi: _log_sigmoid(_dot(glr_ref[bi], wg2_ref[...]) + bg_ref[...]) * inv)
    glog_t = each(lambda bi: glog[bi].T)

    dec_all = each(lambda bi: jnp.exp(_dot(a_ref[...], jnp.concatenate(_split_hi_lo(glog[bi]), axis=0))).astype(BF16))
    et_all = each(lambda bi: _dot(jnp.concatenate(_split_hi_lo(glog_t[bi]), axis=1), akt_ref[...]))

    qe = each(lambda bi: q8[bi] * dec_all[bi][GLA_LEVELS * t:(GLA_LEVELS + 1) * t])
    groups = range(nh * dk // LANES)
    o_inter = each(lambda bi: jnp.concatenate(
        [_dot(qe[bi][:, g * LANES:(g + 1) * LANES], sbd_ref[bi, g]) for g in groups], axis=1))

    per_group = LANES // dk
    lane_head = lax.broadcasted_iota(jnp.int32, (t, LANES), 1) // dk
    head_masks = [lane_head == h for h in range(per_group)]
    lvl = lvl_ref[...]
    attn = each(lambda bi: [jnp.zeros((t, t), F32) for _ in range(nh)])
    for lev in range(GLA_LEVELS + 1):
        m = lvl == lev
        for bi in rows:
            if lev < GLA_LEVELS:
                e = dec_all[bi][lev * t:(lev + 1) * t]
                ql = q8[bi] * e
                kl = k[bi] * e
            else:
                ql = q8[bi]
                kl = k[bi]
            for g in range(nh // per_group):
                lanes = slice(g * LANES, (g + 1) * LANES)
                klg = kl[:, lanes]
                kbd = jnp.concatenate([jnp.where(hm, klg, jnp.zeros_like(klg)) for hm in head_masks], axis=0)
                sc = _dot_nt(ql[:, lanes], kbd)
                for j in range(per_group):
                    h = g * per_group + j
                    attn[bi][h] = jnp.where(m, sc[:, j * t:(j + 1) * t], attn[bi][h])

    nw = nw_ref[...]
    for bi in rows:
        outs = []
        for h in range(nh):
            cols = slice(h * dv, (h + 1) * dv)
            outs.append(_rms(_dot(attn[bi][h].astype(BF16), v[bi][:, cols]) + o_inter[bi][:, cols], nw))
        o_ref[bi] = (jnp.concatenate(outs, axis=1) * sr_ref[bi].astype(F32)).astype(BF16)

    k_tail_t = each(lambda bi: (k[bi].astype(F32).T * jnp.exp(et_all[bi][:, :t])).astype(BF16))
    dec_t = each(lambda bi: jnp.exp(et_all[bi][:, t:]))
    for bi in rows:
        for h in range(nh):
            ch = slice(h * dk, (h + 1) * dk)
            s_new = dec_t[bi][ch, :] * s_old[bi][h] + _dot(k_tail_t[bi][ch, :], v[bi][:, h * dv:(h + 1) * dv])
            s_ref[bi, h] = s_new
            j = h % (LANES // dk)
            sbd_ref[bi, h * dk // LANES, j * dk:(j + 1) * dk, j * dv:(j + 1) * dv] = s_new.astype(BF16)


def _gla(proj, wg2, bg, norm_w):
    b, s, _ = proj.shape
    t = GLA_T
    hk = GLA_HEADS * GLA_DK
    hv = GLA_HEADS * GLA_DV
    a_all, akt, level = _gla_constants()
    wg2_p = jnp.zeros((LANES, hk), F32).at[:GLA_LOW_RANK].set(wg2).astype(BF16)
    const = lambda *shape: pl.BlockSpec(shape, lambda i, j: (0,) * len(shape))
    nb = GLA_BATCH_PER_STEP if b % GLA_BATCH_PER_STEP == 0 else 1
    return pl.pallas_call(
        _gla_kernel,
        out_shape=jax.ShapeDtypeStruct((b, s, hv), BF16),
        grid=(b // nb, s // t),
        in_specs=[pl.BlockSpec((nb, t, hk), lambda i, j: (i, j, _OFF["gq"] // hk)),
                  pl.BlockSpec((nb, t, hk), lambda i, j: (i, j, _OFF["gk"] // hk)),
                  pl.BlockSpec((nb, t, hv), lambda i, j: (i, j, _OFF["gv"] // hv)),
                  pl.BlockSpec((nb, t, hv), lambda i, j: (i, j, _OFF["gr"] // hv)),
                  pl.BlockSpec((nb, t, LANES), lambda i, j: (i, j, _OFF["glr"] // LANES)),
                  const(LANES, hk), const(1, hk),
                  const((GLA_LEVELS + 1) * t, 2 * t), const(2 * t, 2 * t), const(t, t),
                  const(1, GLA_DV)],
        out_specs=pl.BlockSpec((nb, t, hv), lambda i, j: (i, j, 0)),
        scratch_shapes=[pltpu.VMEM((nb, GLA_HEADS, GLA_DK, GLA_DV), F32),
                        pltpu.VMEM((nb, hk // LANES, LANES, LANES // GLA_DK * GLA_DV), BF16)],
        compiler_params=_params("parallel", "arbitrary"),
        name="gla",
    )(proj, proj, proj, proj, proj, wg2_p, bg.reshape(1, hk),
      jnp.asarray(a_all, BF16), jnp.asarray(akt, BF16), jnp.asarray(level), norm_w.reshape(1, GLA_DV))


def _t5_bucket_np(dist):
    n = np.maximum(dist, 0)
    large = REL_MAX_EXACT + (np.log(np.maximum(n, 1).astype(np.float32) / REL_MAX_EXACT)
                             / math.log(REL_MAX_DIST / REL_MAX_EXACT)
                             * (REL_BUCKETS - REL_MAX_EXACT)).astype(np.int32)
    large = np.minimum(large, REL_BUCKETS - 1)
    return np.where(n < REL_MAX_EXACT, n, large).astype(np.int32)


def _bias_kernel(rb_ref, bk_ref, o_ref):
    h = pl.program_id(0)
    for which in range(2):
        bk = bk_ref[which]
        acc = jnp.full(bk.shape, NEG_INF, F32)
        for bucket in range(REL_BUCKETS):
            acc = jnp.where(bk == bucket, rb_ref[bucket, h] * LOG2E, acc)
        o_ref[0, which] = acc


def _bias_tables(rel_bias):
    blk = MOBA_BLOCK
    j = np.arange(blk)[:, None]
    i = np.arange(blk)[None, :]
    own = np.where(i >= j, _t5_bucket_np(i - j), -1)
    prev = _t5_bucket_np(i + blk - j)
    buckets = np.stack([own, prev]).astype(np.int32)
    return pl.pallas_call(
        _bias_kernel,
        out_shape=jax.ShapeDtypeStruct((MOBA_HEADS, 2, blk, blk), F32),
        grid=(MOBA_HEADS,),
        in_specs=[pl.BlockSpec(memory_space=pltpu.SMEM),
                  pl.BlockSpec((2, blk, blk), lambda h: (0, 0, 0))],
        out_specs=pl.BlockSpec((1, 2, blk, blk), lambda h: (h, 0, 0, 0)),
        compiler_params=_params("parallel"),
        name="moba_bias",
    )(rel_bias, jnp.asarray(buckets))


def _moba_kernel(rb_ref, q_ref, k_ref, v_ref, km_ref, bias_ref, o_ref, vt_ref, sel_ref):
    blk, dh, nh = MOBA_BLOCK, MOBA_HEAD_DIM, MOBA_HEADS
    nb = k_ref.shape[1] // blk
    qb = pl.program_id(1)
    chains = [(bi, h) for bi in range(q_ref.shape[0]) for h in range(nh)]
    each = lambda f: [f(c, bi, h) for c, (bi, h) in enumerate(chains)]
    head = lambda h: slice(h * dh, (h + 1) * dh)
    keys = lambda bi, h, n: k_ref[bi, pl.ds(pl.multiple_of(n * blk, blk), blk), head(h)]

    @pl.when(qb == 0)
    def _():
        ones = jnp.ones((BF16_SUBLANES, blk), BF16)
        for c, (bi, h) in enumerate(chains):
            for n in range(nb):
                vt = v_ref[bi, n * blk:(n + 1) * blk, head(h)].astype(F32).T.astype(BF16)
                vt_ref[c, n] = jnp.concatenate([vt, ones], axis=0)

    n_iota = lax.broadcasted_iota(jnp.int32, (nb, blk), 0)
    q_sc = each(lambda c, bi, h: q_ref[bi, :, head(h)])

    s_own = each(lambda c, bi, h: _dot_nt(keys(bi, h, qb), q_sc[c]) + bias_ref[h, 0])
    m0 = each(lambda c, bi, h: jnp.max(s_own[c], axis=0, keepdims=True))
    p0 = each(lambda c, bi, h: jnp.exp2(s_own[c] - m0[c]))
    carry = each(lambda c, bi, h: (m0[c], _dot(vt_ref[c, qb], p0[c].astype(BF16))))

    for c, (bi, h) in enumerate(chains):
        km_hi, km_lo = _split_hi_lo(km_ref[bi * nb:(bi + 1) * nb, 0, head(h)])
        sc = _dot_nt(km_hi, q_sc[c]) + _dot_nt(km_lo, q_sc[c])
        rank = jnp.zeros((nb, blk), jnp.int32)
        for m in range(nb):
            sm = sc[m:m + 1, :]
            beats = ((sm > sc) | ((sm == sc) & (m < n_iota))) & (m < qb)
            rank = rank + beats.astype(jnp.int32)
        sel_ref[c] = jnp.where((n_iota < qb) & (rank < MOBA_TOPK), 0.0, NEG_INF)

    def attend(n, tile_bias, row_bias, carry):
        s = each(lambda c, bi, h: _dot_nt(keys(bi, h, n), q_sc[c]))
        if tile_bias is not None:
            s = each(lambda c, bi, h: s[c] + tile_bias(c, h))
        rowb = each(lambda c, bi, h: row_bias(c, h))
        m_new = each(lambda c, bi, h: jnp.maximum(carry[c][0], jnp.max(s[c], axis=0, keepdims=True) + rowb[c]))
        alpha = each(lambda c, bi, h: jnp.exp2(carry[c][0] - m_new[c]))
        p = each(lambda c, bi, h: jnp.exp2(s[c] - (m_new[c] - rowb[c])))
        pv = each(lambda c, bi, h: _dot(vt_ref[c, n], p[c].astype(BF16)))
        return each(lambda c, bi, h: (m_new[c], alpha[c] * carry[c][1] + pv[c]))

    prev = jnp.maximum(qb - 1, 0)
    carry = attend(prev, lambda c, h: bias_ref[h, 1], lambda c, h: sel_ref[c, pl.ds(prev, 1), :], carry)
    carry = lax.fori_loop(
        0, prev,
        lambda n, cr: attend(n, None, lambda c, h: sel_ref[c, pl.ds(n, 1), :] + rb_ref[REL_BUCKETS - 1, h] * LOG2E,
                             cr), carry)
    outs = each(lambda c, bi, h: (carry[c][1][:dh] / carry[c][1][dh:dh + 1]).T)
    for bi in range(q_ref.shape[0]):
        o_ref[bi] = jnp.concatenate(outs[bi * nh:(bi + 1) * nh], axis=1).astype(BF16)


def _moba(proj, kmean, rel_bias, bias_tab):
    b, s, _ = proj.shape
    blk, dh, nh = MOBA_BLOCK, MOBA_HEAD_DIM, MOBA_HEADS
    c = nh * dh
    nb = s // blk
    rows = MOBA_BATCH_PER_STEP if b % MOBA_BATCH_PER_STEP == 0 else 1
    nc = rows * nh
    return pl.pallas_call(
        _moba_kernel,
        out_shape=jax.ShapeDtypeStruct((b, s, c), BF16),
        grid=(b // rows, nb),
        in_specs=[pl.BlockSpec(memory_space=pltpu.SMEM),
                  pl.BlockSpec((rows, blk, c), lambda i, j: (i, j, _OFF["mq"] // c)),
                  pl.BlockSpec((rows, s, c), lambda i, j: (i, 0, _OFF["mk"] // c), pipeline_mode=pl.Buffered(1)),
                  pl.BlockSpec((rows, s, c), lambda i, j: (i, 0, _OFF["mv"] // c), pipeline_mode=pl.Buffered(1)),
                  pl.BlockSpec((rows * nb, 1, c), lambda i, j: (i, 0, 0)),
                  pl.BlockSpec((nh, 2, blk, blk), lambda i, j: (0, 0, 0, 0))],
        out_specs=pl.BlockSpec((rows, blk, c), lambda i, j: (i, j, 0)),
        scratch_shapes=[pltpu.VMEM((nc, nb, dh + BF16_SUBLANES, blk), BF16), pltpu.VMEM((nc, nb, blk), F32)],
        compiler_params=_params("parallel", "arbitrary"),
        name="moba",
    )(rel_bias, proj, proj, proj, kmean, bias_tab)


def _pool_branch(cur, halo, pw_ref, ps_ref, start):
    tm = cur.shape[0]
    ext = jnp.concatenate([halo, cur], axis=0)
    pos = start + lax.broadcasted_iota(jnp.int32, (tm, 1), 0)
    outs = []
    for g, w in enumerate(POOL_WINDOWS):
        cols = slice(g * POOL_GROUP_DIM, (g + 1) * POOL_GROUP_DIM)
        a = ext[:, cols]
        shift = 1
        while shift < w:
            a = a + pltpu.roll(a, shift, 0)
            shift *= 2
        cnt = jnp.minimum(pos + 1, w).astype(F32)
        p = a[POOL_HALO:, :] / cnt - cur[:, cols]
        outs.append(_dot(p.astype(BF16), pw_ref[g]))
    return jnp.concatenate(outs, axis=1) * ps_ref[...]


def _mix_kernel(seq, layer, x_ref, ya_ref, yc_ref, u_ref, halo_ref, ga0_ref, ga1_ref, gb0_ref, gb1_ref, gc0_ref,
                gc1_ref, pw_ref, ps_ref, n2_ref, wa_hbm, wb_hbm, wc_hbm, wo_hbm, wg_hbm, wu_hbm, wd_hbm, o_ref,
                wa_ref, wb_ref, wc_ref, wo_ref, wg_ref, wu_ref, wd_ref, stage_ref, sem):
    @pl.when(pl.program_id(0) == 0)
    def _():
        def into(dst):
            def store(r0, w):
                dst[r0:r0 + w.shape[0], :] = w
            return store

        _stage_weights(layer, [(wa_hbm, into(wa_ref)), (wb_hbm, into(wb_ref)), (wc_hbm, into(wc_ref)),
                               (wo_hbm, into(wo_ref)), (wg_hbm, into(wg_ref)), (wu_hbm, into(wu_ref)),
                               (wd_hbm, into(wd_ref))], stage_ref, sem)

    tm = x_ref.shape[0]
    start = (pl.program_id(0) * tm) % seq
    halo = jnp.where(start > 0, halo_ref[...].astype(F32), 0.0)
    yb = _pool_branch(u_ref[...].astype(F32), halo, pw_ref, ps_ref, start)
    gate = lambda lo_ref, hi_ref: _sigmoid(jnp.concatenate([lo_ref[...], hi_ref[...]], axis=1).astype(F32))
    merged = (gate(ga0_ref, ga1_ref) * _dot(ya_ref[...], wa_ref[...])
              + gate(gb0_ref, gb1_ref) * _dot(yb.astype(BF16), wb_ref[...])
              + gate(gc0_ref, gc1_ref) * _dot(yc_ref[...], wc_ref[...]))
    x1 = x_ref[...] + _dot(merged.astype(BF16), wo_ref[...])
    h = _rms(x1, n2_ref[...]).astype(BF16)
    acc = x1
    for lo in range(0, D_FF, FFN_CHUNK):
        cols = slice(lo, min(lo + FFN_CHUNK, D_FF))
        g = _dot(h, wg_ref[:, cols])
        u = _dot(h, wu_ref[:, cols])
        act = (g * _sigmoid(g) * u).astype(BF16)
        acc = acc + _dot(act, wd_ref[cols, :])
    o_ref[...] = acc


def _mix(x2d, ya, yc, proj2d, seq, layer, pool_w, pool_scale, norm2_w, wa, wb, wc, wo, wg, wu, wd, tm):
    m = x2d.shape[0]
    c = BRANCH_WIDTH
    assert seq % tm == 0 and tm % POOL_HALO == 0 and POOL_HALO == BF16_SUBLANES
    weights = (wa, wb, wc, wo, wg, wu, wd)
    assert all(w.shape[1] % W_STAGE_ROWS == 0 for w in weights)
    row = lambda w, col: pl.BlockSpec((tm, w), lambda i: (i, col))
    resident = lambda *shape: pl.BlockSpec((None,) + shape, lambda i: (layer,) + (0,) * len(shape),
                                           pipeline_mode=pl.Buffered(1))
    halo_blocks = tm // POOL_HALO
    gates = [row(c, _OFF[g] // c + half) for g in ("ga", "gb", "gc") for half in range(D_MODEL // c)]
    return pl.pallas_call(
        functools.partial(_mix_kernel, seq, layer),
        out_shape=jax.ShapeDtypeStruct((m, D_MODEL), F32),
        grid=(m // tm,),
        in_specs=[row(D_MODEL, 0), row(c, 0), row(c, 0), row(c, _OFF["pu"] // c),
                  pl.BlockSpec((POOL_HALO, c), lambda i: (jnp.maximum(i * halo_blocks - 1, 0), _OFF["pu"] // c)),
                  *gates,
                  resident(POOL_GROUPS, POOL_GROUP_DIM, POOL_GROUP_DIM), resident(1, c), resident(1, D_MODEL),
                  *([pl.BlockSpec(memory_space=pl.ANY)] * len(weights))],
        out_specs=row(D_MODEL, 0),
        scratch_shapes=[*(pltpu.VMEM(w.shape[1:], BF16) for w in weights),
                        pltpu.VMEM((2, W_STAGE_ROWS, max(w.shape[2] for w in weights)), F32),
                        pltpu.SemaphoreType.DMA((2,))],
        compiler_params=_params("arbitrary"),
        name="mix",
    )(x2d, ya, yc, proj2d, proj2d, *([proj2d] * len(gates)), pool_w, pool_scale, norm2_w, *weights)


def _row_tile(m, want):
    t = min(m, want)
    assert m % t == 0
    return t


def kernel(x, norm1_w, w_in, gla_wg2, gla_bg, gla_norm_w, pool_w, pool_scale, moba_qn_w, moba_kn_w,
           rel_bias, w_up_a, w_up_b, w_up_c, w_out, norm2_w, ffn_w_gate, ffn_w_up, ffn_w_down):
    b, s, d = x.shape
    assert d == D_MODEL and s % MOBA_BLOCK == 0 and s % GLA_T == 0
    m = b * s
    bias_tab = _bias_tables(rel_bias)
    norm1 = norm1_w.reshape(DEPTH, 1, d)
    qn = moba_qn_w.reshape(DEPTH, 1, MOBA_HEAD_DIM)
    kn = moba_kn_w.reshape(DEPTH, 1, MOBA_HEAD_DIM)
    mix_weights = (pool_w.astype(BF16), pool_scale.reshape(DEPTH, 1, -1), norm2_w.reshape(DEPTH, 1, d),
                   w_up_a, w_up_b, w_up_c, w_out, ffn_w_gate, ffn_w_up, ffn_w_down)
    x2d = x.reshape(m, d)
    for l in range(DEPTH):
        proj2d, kmean = _inproj(x2d, norm1, w_in, qn, kn, l, _row_tile(m, 512))
        proj = proj2d.reshape(b, s, PROJ_COLS)
        ya = _gla(proj, gla_wg2[l], gla_bg[l], gla_norm_w[l])
        yc = _moba(proj, kmean, rel_bias, bias_tab)
        x2d = _mix(x2d, ya.reshape(m, -1), yc.reshape(m, -1), proj2d, s, l, *mix_weights, _row_tile(s, 512))
    return x2d.reshape(b, s, d)
```

```python
import functools
import math

import numpy as np
import jax
import jax.numpy as jnp
from jax import lax
from jax.experimental import pallas as pl
from jax.experimental.pallas import tpu as pltpu

D_MODEL = 1024
DEPTH = 2
BRANCH_WIDTH = D_MODEL // 2
GLA_HEADS = 4
GLA_DV = BRANCH_WIDTH // GLA_HEADS
GLA_DK = GLA_DV // 2
GLA_LOW_RANK = 16
GLA_GATE_NORMALIZER = 16.0
POOL_WINDOWS = (2, 4, 8, 16)
POOL_GROUPS = 4
POOL_GROUP_DIM = BRANCH_WIDTH // POOL_GROUPS
MOBA_HEADS = 4
MOBA_HEAD_DIM = BRANCH_WIDTH // MOBA_HEADS
MOBA_BLOCK = 256
MOBA_TOPK = 3
REL_BUCKETS = 32
REL_MAX_EXACT = REL_BUCKETS // 2
REL_MAX_DIST = 128
D_FF = ((8 * D_MODEL + 767) // 768) * 256
NORM_EPS = 1e-6
NEG_INF = -1e30
LOG2E = math.log2(math.e)

LANES = 128
BF16_SUBLANES = 16
VMEM_LIMIT_BYTES = 56 * 1024 * 1024

_SECTIONS = (("gq", 256), ("gk", 256), ("gv", 512), ("glr", GLA_LOW_RANK), ("gr", 512), ("pu", 512),
             ("mq", 512), ("mk", 512), ("mv", 512), ("ga", 1024), ("gb", 1024), ("gc", 1024))
_OFF = {}
_o = 0
for _n, _w in _SECTIONS:
    if _n != "glr":
        assert _o % min(_w, 512) == 0
        _OFF[_n] = _o
        _o += _w
_OFF["glr"] = _o
PROJ_COLS = _o + LANES
GLR_SRC = sum(w for n, w in _SECTIONS[:3])
PROJ_CHUNK = 512
CAST_STEPS = 16

GLA_T = 128
GLA_LEVELS = 7
GLA_BATCH_PER_STEP = 4
MOBA_BATCH_PER_STEP = 4
POOL_HALO = 16
FFN_CHUNK = 256

F32 = jnp.float32
BF16 = jnp.bfloat16


def _params(*sem):
    return pltpu.CompilerParams(dimension_semantics=sem, vmem_limit_bytes=VMEM_LIMIT_BYTES)


def _dot(a, b):
    return jnp.dot(a, b, preferred_element_type=F32)


def _dot_nt(a, b):
    return lax.dot_general(a, b, (((1,), (1,)), ((), ())), preferred_element_type=F32)


def _split_hi_lo(x):
    hi = x.astype(BF16)
    lo = (x - hi.astype(F32)).astype(BF16)
    return hi, lo


def _rms(x, w):
    return x * lax.rsqrt(jnp.mean(x * x, axis=-1, keepdims=True) + NORM_EPS) * w


def _sigmoid(x):
    return 1.0 / (1.0 + jnp.exp(-x))


def _head_rms(y, w, scale=1.0):
    dh = MOBA_HEAD_DIM
    return jnp.concatenate([_rms(y[:, h * dh:(h + 1) * dh], w) * scale for h in range(y.shape[1] // dh)], axis=1)


def _inproj_kernel(x_ref, nw_ref, w_ref, qn_ref, kn_ref, o_ref, km_ref, wp_ref):
    @pl.when(pl.program_id(0) == 0)
    def _():
        glr_end = GLR_SRC + GLA_LOW_RANK
        wp_ref[:, :GLR_SRC] = w_ref[:, :GLR_SRC]
        wp_ref[:, GLR_SRC:_OFF["glr"]] = w_ref[:, glr_end:]
        wp_ref[:, _OFF["glr"]:] = jnp.concatenate(
            [w_ref[:, GLR_SRC:glr_end], jnp.zeros((D_MODEL, LANES - GLA_LOW_RANK), BF16)], axis=1)

    h = _rms(x_ref[...], nw_ref[...]).astype(BF16)
    for lo in range(0, PROJ_COLS, PROJ_CHUNK):
        cols = slice(lo, min(lo + PROJ_CHUNK, PROJ_COLS))
        y = _dot(h, wp_ref[:, cols])
        if lo == _OFF["gr"]:
            y = y * _sigmoid(y)
        elif lo == _OFF["mq"]:
            y = _head_rms(y, qn_ref[...], MOBA_HEAD_DIM ** -0.5 * LOG2E)
        elif lo == _OFF["mk"]:
            y = _head_rms(y, kn_ref[...])
            for j in range(y.shape[0] // MOBA_BLOCK):
                km_ref[j] = jnp.mean(y[j * MOBA_BLOCK:(j + 1) * MOBA_BLOCK], axis=0, keepdims=True)
        o_ref[:, cols] = y.astype(BF16)


def _inproj(x2d, norm_w, w_bf, qn_w, kn_w, layer, tm):
    m = x2d.shape[0]
    in_cols = w_bf.shape[-1]
    assert tm % MOBA_BLOCK == 0 and all(_OFF[n] % PROJ_CHUNK == 0 for n in ("gr", "mq", "mk"))
    layer_block = lambda *shape: pl.BlockSpec((None,) + shape, lambda i: (layer,) + (0,) * len(shape))
    return pl.pallas_call(
        _inproj_kernel,
        out_shape=(jax.ShapeDtypeStruct((m, PROJ_COLS), BF16),
                   jax.ShapeDtypeStruct((m // MOBA_BLOCK, 1, BRANCH_WIDTH), F32)),
        grid=(m // tm,),
        in_specs=[pl.BlockSpec((tm, D_MODEL), lambda i: (i, 0)),
                  layer_block(1, D_MODEL),
                  pl.BlockSpec((None, D_MODEL, in_cols), lambda i: (layer, 0, 0), pipeline_mode=pl.Buffered(1)),
                  layer_block(1, MOBA_HEAD_DIM), layer_block(1, MOBA_HEAD_DIM)],
        out_specs=(pl.BlockSpec((tm, PROJ_COLS), lambda i: (i, 0)),
                   pl.BlockSpec((tm // MOBA_BLOCK, 1, BRANCH_WIDTH), lambda i: (i, 0, 0))),
        scratch_shapes=[pltpu.VMEM((D_MODEL, PROJ_COLS), BF16)],
        compiler_params=_params("arbitrary"),
        name="inproj",
    )(x2d, norm_w, w_bf, qn_w, kn_w)


def _gla_constants():
    t = np.arange(GLA_T)
    u = np.arange(GLA_T)
    a_rows = []
    level = np.full((GLA_T, GLA_T), -1, np.int32)
    for lev in range(GLA_LEVELS):
        s = (GLA_T // 2) >> lev
        g = 2 * s
        first = (t % g) < s
        ref = (t // g) * g + s - 1
        a = np.where(first[:, None],
                     (u[None, :] > t[:, None]) & (u[None, :] <= ref[:, None]),
                     (u[None, :] > ref[:, None]) & (u[None, :] <= t[:, None]))
        a_rows.append(a)
        same = (t[:, None] // g) == (t[None, :] // g)
        level[same & (~first)[:, None] & first[None, :]] = lev
    a_rows.append(u[None, :] <= t[:, None])
    level[np.eye(GLA_T, dtype=bool)] = GLA_LEVELS
    a_all = np.concatenate(a_rows, axis=0).astype(np.float32)
    akt = np.concatenate([(u[:, None] > t[None, :]), np.ones((GLA_T, GLA_T), bool)], axis=1).astype(np.float32)
    return np.concatenate([a_all, a_all], axis=1), np.concatenate([akt, akt], axis=0), level


def _log_sigmoid(z):
    return jnp.minimum(z, 0.0) - jnp.log1p(jnp.exp(-jnp.abs(z)))


def _gla_kernel(q_ref, k_ref, v_ref, sr_ref, glr_ref, wg2_ref, bg_ref, a_ref, akt_ref, lvl_ref, nw_ref, o_ref,
                s_ref, sbd_ref):
    nh, dk, dv, t = GLA_HEADS, GLA_DK, GLA_DV, GLA_T

    @pl.when(pl.program_id(1) == 0)
    def _():
        s_ref[...] = jnp.zeros_like(s_ref)
        sbd_ref[...] = jnp.zeros_like(sbd_ref)

    rows = range(q_ref.shape[0])
    each = lambda f: [f(bi) for bi in rows]
    q8 = each(lambda bi: q_ref[bi] * (dk ** -0.5))
    k = each(lambda bi: k_ref[bi])
    v = each(lambda bi: v_ref[bi])
    s_old = each(lambda bi: [s_ref[bi, h] for h in range(nh)])

    inv = 1.0 / GLA_GATE_NORMALIZER
    glog = each(lambda bi: _log_sigmoid(_dot(glr_ref[bi], wg2_ref[...]) + bg_ref[...]) * inv)
    glog_t = each(lambda bi: glog[bi].T)

    dec_all = each(lambda bi: jnp.exp(_dot(a_ref[...], jnp.concatenate(_split_hi_lo(glog[bi]), axis=0))).astype(BF16))
    et_all = each(lambda bi: _dot(jnp.concatenate(_split_hi_lo(glog_t[bi]), axis=1), akt_ref[...]))

    qe = each(lambda bi: q8[bi] * dec_all[bi][GLA_LEVELS * t:(GLA_LEVELS + 1) * t])
    groups = range(nh * dk // LANES)
    o_inter = each(lambda bi: jnp.concatenate(
        [_dot(qe[bi][:, g * LANES:(g + 1) * LANES], sbd_ref[bi, g]) for g in groups], axis=1))

    per_group = LANES // dk
    lane_head = lax.broadcasted_iota(jnp.int32, (t, LANES), 1) // dk
    head_masks = [lane_head == h for h in range(per_group)]
    lvl = lvl_ref[...]
    attn = each(lambda bi: [jnp.zeros((t, t), F32) for _ in range(nh)])
    for lev in range(GLA_LEVELS + 1):
        m = lvl == lev
        for bi in rows:
            if lev < GLA_LEVELS:
                e = dec_all[bi][lev * t:(lev + 1) * t]
                ql = q8[bi] * e
                kl = k[bi] * e
            else:
                ql = q8[bi]
                kl = k[bi]
            for g in range(nh // per_group):
                lanes = slice(g * LANES, (g + 1) * LANES)
                klg = kl[:, lanes]
                kbd = jnp.concatenate([jnp.where(hm, klg, jnp.zeros_like(klg)) for hm in head_masks], axis=0)
                sc = _dot_nt(ql[:, lanes], kbd)
                for j in range(per_group):
                    h = g * per_group + j
                    attn[bi][h] = jnp.where(m, sc[:, j * t:(j + 1) * t], attn[bi][h])

    nw = nw_ref[...]
    for bi in rows:
        outs = []
        for h in range(nh):
            cols = slice(h * dv, (h + 1) * dv)
            outs.append(_rms(_dot(attn[bi][h].astype(BF16), v[bi][:, cols]) + o_inter[bi][:, cols], nw))
        o_ref[bi] = (jnp.concatenate(outs, axis=1) * sr_ref[bi].astype(F32)).astype(BF16)

    k_tail_t = each(lambda bi: (k[bi].astype(F32).T * jnp.exp(et_all[bi][:, :t])).astype(BF16))
    dec_t = each(lambda bi: jnp.exp(et_all[bi][:, t:]))
    for bi in rows:
        for h in range(nh):
            ch = slice(h * dk, (h + 1) * dk)
            s_new = dec_t[bi][ch, :] * s_old[bi][h] + _dot(k_tail_t[bi][ch, :], v[bi][:, h * dv:(h + 1) * dv])
            s_ref[bi, h] = s_new
            j = h % (LANES // dk)
            sbd_ref[bi, h * dk // LANES, j * dk:(j + 1) * dk, j * dv:(j + 1) * dv] = s_new.astype(BF16)


def _gla(proj, wg2, bg, norm_w):
    b, s, _ = proj.shape
    t = GLA_T
    hk = GLA_HEADS * GLA_DK
    hv = GLA_HEADS * GLA_DV
    a_all, akt, level = _gla_constants()
    wg2_p = jnp.zeros((LANES, hk), F32).at[:GLA_LOW_RANK].set(wg2).astype(BF16)
    const = lambda *shape: pl.BlockSpec(shape, lambda i, j: (0,) * len(shape))
    nb = GLA_BATCH_PER_STEP if b % GLA_BATCH_PER_STEP == 0 else 1
    return pl.pallas_call(
        _gla_kernel,
        out_shape=jax.ShapeDtypeStruct((b, s, hv), BF16),
        grid=(b // nb, s // t),
        in_specs=[pl.BlockSpec((nb, t, hk), lambda i, j: (i, j, _OFF["gq"] // hk)),
                  pl.BlockSpec((nb, t, hk), lambda i, j: (i, j, _OFF["gk"] // hk)),
                  pl.BlockSpec((nb, t, hv), lambda i, j: (i, j, _OFF["gv"] // hv)),
                  pl.BlockSpec((nb, t, hv), lambda i, j: (i, j, _OFF["gr"] // hv)),
                  pl.BlockSpec((nb, t, LANES), lambda i, j: (i, j, _OFF["glr"] // LANES)),
                  const(LANES, hk), const(1, hk),
                  const((GLA_LEVELS + 1) * t, 2 * t), const(2 * t, 2 * t), const(t, t),
                  const(1, GLA_DV)],
        out_specs=pl.BlockSpec((nb, t, hv), lambda i, j: (i, j, 0)),
        scratch_shapes=[pltpu.VMEM((nb, GLA_HEADS, GLA_DK, GLA_DV), F32),
                        pltpu.VMEM((nb, hk // LANES, LANES, LANES // GLA_DK * GLA_DV), BF16)],
        compiler_params=_params("parallel", "arbitrary"),
        name="gla",
    )(proj, proj, proj, proj, proj, wg2_p, bg.reshape(1, hk),
      jnp.asarray(a_all, BF16), jnp.asarray(akt, BF16), jnp.asarray(level), norm_w.reshape(1, GLA_DV))


def _t5_bucket_np(dist):
    n = np.maximum(dist, 0)
    large = REL_MAX_EXACT + (np.log(np.maximum(n, 1).astype(np.float32) / REL_MAX_EXACT)
                             / math.log(REL_MAX_DIST / REL_MAX_EXACT)
                             * (REL_BUCKETS - REL_MAX_EXACT)).astype(np.int32)
    large = np.minimum(large, REL_BUCKETS - 1)
    return np.where(n < REL_MAX_EXACT, n, large).astype(np.int32)


def _bias_kernel(rb_ref, bk_ref, o_ref):
    h = pl.program_id(0)
    for which in range(2):
        bk = bk_ref[which]
        acc = jnp.full(bk.shape, NEG_INF, F32)
        for bucket in range(REL_BUCKETS):
            acc = jnp.where(bk == bucket, rb_ref[bucket, h] * LOG2E, acc)
        o_ref[0, which] = acc


def _bias_tables(rel_bias):
    blk = MOBA_BLOCK
    j = np.arange(blk)[:, None]
    i = np.arange(blk)[None, :]
    own = np.where(i >= j, _t5_bucket_np(i - j), -1)
    prev = _t5_bucket_np(i + blk - j)
    buckets = np.stack([own, prev]).astype(np.int32)
    return pl.pallas_call(
        _bias_kernel,
        out_shape=jax.ShapeDtypeStruct((MOBA_HEADS, 2, blk, blk), F32),
        grid=(MOBA_HEADS,),
        in_specs=[pl.BlockSpec(memory_space=pltpu.SMEM),
                  pl.BlockSpec((2, blk, blk), lambda h: (0, 0, 0))],
        out_specs=pl.BlockSpec((1, 2, blk, blk), lambda h: (h, 0, 0, 0)),
        compiler_params=_params("parallel"),
        name="moba_bias",
    )(rel_bias, jnp.asarray(buckets))


def _moba_kernel(rb_ref, q_ref, k_ref, v_ref, km_ref, bias_ref, o_ref, vt_ref, sel_ref):
    blk, dh, nh = MOBA_BLOCK, MOBA_HEAD_DIM, MOBA_HEADS
    nb = k_ref.shape[1] // blk
    qb = pl.program_id(1)
    chains = [(bi, h) for bi in range(q_ref.shape[0]) for h in range(nh)]
    each = lambda f: [f(c, bi, h) for c, (bi, h) in enumerate(chains)]
    head = lambda h: slice(h * dh, (h + 1) * dh)
    keys = lambda bi, h, n: k_ref[bi, pl.ds(pl.multiple_of(n * blk, blk), blk), head(h)]

    @pl.when(qb == 0)
    def _():
        ones = jnp.ones((BF16_SUBLANES, blk), BF16)
        for c, (bi, h) in enumerate(chains):
            for n in range(nb):
                vt = v_ref[bi, n * blk:(n + 1) * blk, head(h)].astype(F32).T.astype(BF16)
                vt_ref[c, n] = jnp.concatenate([vt, ones], axis=0)

    n_iota = lax.broadcasted_iota(jnp.int32, (nb, blk), 0)
    q_sc = each(lambda c, bi, h: q_ref[bi, :, head(h)])

    s_own = each(lambda c, bi, h: _dot_nt(keys(bi, h, qb), q_sc[c]) + bias_ref[h, 0])
    m0 = each(lambda c, bi, h: jnp.max(s_own[c], axis=0, keepdims=True))
    p0 = each(lambda c, bi, h: jnp.exp2(s_own[c] - m0[c]))
    carry = each(lambda c, bi, h: (m0[c], _dot(vt_ref[c, qb], p0[c].astype(BF16))))

    for c, (bi, h) in enumerate(chains):
        km_hi, km_lo = _split_hi_lo(km_ref[bi * nb:(bi + 1) * nb, 0, head(h)])
        sc = _dot_nt(km_hi, q_sc[c]) + _dot_nt(km_lo, q_sc[c])
        rank = jnp.zeros((nb, blk), jnp.int32)
        for m in range(nb):
            sm = sc[m:m + 1, :]
            beats = ((sm > sc) | ((sm == sc) & (m < n_iota))) & (m < qb)
            rank = rank + beats.astype(jnp.int32)
        sel_ref[c] = jnp.where((n_iota < qb) & (rank < MOBA_TOPK), 0.0, NEG_INF)

    def attend(n, tile_bias, row_bias, carry):
        s = each(lambda c, bi, h: _dot_nt(keys(bi, h, n), q_sc[c]))
        if tile_bias is not None:
            s = each(lambda c, bi, h: s[c] + tile_bias(c, h))
        rowb = each(lambda c, bi, h: row_bias(c, h))
        m_new = each(lambda c, bi, h: jnp.maximum(carry[c][0], jnp.max(s[c], axis=0, keepdims=True) + rowb[c]))
        alpha = each(lambda c, bi, h: jnp.exp2(carry[c][0] - m_new[c]))
        p = each(lambda c, bi, h: jnp.exp2(s[c] - (m_new[c] - rowb[c])))
        pv = each(lambda c, bi, h: _dot(vt_ref[c, n], p[c].astype(BF16)))
        return each(lambda c, bi, h: (m_new[c], alpha[c] * carry[c][1] + pv[c]))

    prev = jnp.maximum(qb - 1, 0)
    carry = attend(prev, lambda c, h: bias_ref[h, 1], lambda c, h: sel_ref[c, pl.ds(prev, 1), :], carry)
    carry = lax.fori_loop(
        0, prev,
        lambda n, cr: attend(n, None, lambda c, h: sel_ref[c, pl.ds(n, 1), :] + rb_ref[REL_BUCKETS - 1, h] * LOG2E,
                             cr), carry)
    outs = each(lambda c, bi, h: (carry[c][1][:dh] / carry[c][1][dh:dh + 1]).T)
    for bi in range(q_ref.shape[0]):
        o_ref[bi] = jnp.concatenate(outs[bi * nh:(bi + 1) * nh], axis=1).astype(BF16)


def _moba(proj, kmean, rel_bias, bias_tab):
    b, s, _ = proj.shape
    blk, dh, nh = MOBA_BLOCK, MOBA_HEAD_DIM, MOBA_HEADS
    c = nh * dh
    nb = s // blk
    rows = MOBA_BATCH_PER_STEP if b % MOBA_BATCH_PER_STEP == 0 else 1
    nc = rows * nh
    return pl.pallas_call(
        _moba_kernel,
        out_shape=jax.ShapeDtypeStruct((b, s, c), BF16),
        grid=(b // rows, nb),
        in_specs=[pl.BlockSpec(memory_space=pltpu.SMEM),
                  pl.BlockSpec((rows, blk, c), lambda i, j: (i, j, _OFF["mq"] // c)),
                  pl.BlockSpec((rows, s, c), lambda i, j: (i, 0, _OFF["mk"] // c), pipeline_mode=pl.Buffered(1)),
                  pl.BlockSpec((rows, s, c), lambda i, j: (i, 0, _OFF["mv"] // c), pipeline_mode=pl.Buffered(1)),
                  pl.BlockSpec((rows * nb, 1, c), lambda i, j: (i, 0, 0)),
                  pl.BlockSpec((nh, 2, blk, blk), lambda i, j: (0, 0, 0, 0))],
        out_specs=pl.BlockSpec((rows, blk, c), lambda i, j: (i, j, 0)),
        scratch_shapes=[pltpu.VMEM((nc, nb, dh + BF16_SUBLANES, blk), BF16), pltpu.VMEM((nc, nb, blk), F32)],
        compiler_params=_params("parallel", "arbitrary"),
        name="moba",
    )(rel_bias, proj, proj, proj, kmean, bias_tab)


def _pool_branch(cur, halo, pw_ref, ps_ref, start):
    tm = cur.shape[0]
    ext = jnp.concatenate([halo, cur], axis=0)
    pos = start + lax.broadcasted_iota(jnp.int32, (tm, 1), 0)
    outs = []
    for g, w in enumerate(POOL_WINDOWS):
        cols = slice(g * POOL_GROUP_DIM, (g + 1) * POOL_GROUP_DIM)
        a = ext[:, cols]
        shift = 1
        while shift < w:
            a = a + pltpu.roll(a, shift, 0)
            shift *= 2
        cnt = jnp.minimum(pos + 1, w).astype(F32)
        p = a[POOL_HALO:, :] / cnt - cur[:, cols]
        outs.append(_dot(p.astype(BF16), pw_ref[g]))
    return jnp.concatenate(outs, axis=1) * ps_ref[...]


def _mix_kernel(seq, x_ref, ya_ref, yc_ref, u_ref, halo_ref, ga0_ref, ga1_ref, gb0_ref, gb1_ref, gc0_ref, gc1_ref,
                pw_ref, ps_ref, wa_ref, wb_ref, wc_ref, wo_ref, n2_ref, wg_ref, wu_ref, wd_ref, o_ref):
    tm = x_ref.shape[0]
    start = (pl.program_id(0) * tm) % seq
    halo = jnp.where(start > 0, halo_ref[...].astype(F32), 0.0)
    yb = _pool_branch(u_ref[...].astype(F32), halo, pw_ref, ps_ref, start)
    gate = lambda lo_ref, hi_ref: _sigmoid(jnp.concatenate([lo_ref[...], hi_ref[...]], axis=1).astype(F32))
    merged = (gate(ga0_ref, ga1_ref) * _dot(ya_ref[...], wa_ref[...])
              + gate(gb0_ref, gb1_ref) * _dot(yb.astype(BF16), wb_ref[...])
              + gate(gc0_ref, gc1_ref) * _dot(yc_ref[...], wc_ref[...]))
    x1 = x_ref[...] + _dot(merged.astype(BF16), wo_ref[...])
    h = _rms(x1, n2_ref[...]).astype(BF16)
    acc = x1
    for lo in range(0, D_FF, FFN_CHUNK):
        cols = slice(lo, min(lo + FFN_CHUNK, D_FF))
        g = _dot(h, wg_ref[:, cols])
        u = _dot(h, wu_ref[:, cols])
        act = (g * _sigmoid(g) * u).astype(BF16)
        acc = acc + _dot(act, wd_ref[cols, :])
    o_ref[...] = acc


def _mix(x2d, ya, yc, proj2d, seq, layer, pool_w, pool_scale, wa, wb, wc, wo, norm2_w, wg, wu, wd, tm):
    m = x2d.shape[0]
    c = BRANCH_WIDTH
    assert seq % tm == 0 and tm % POOL_HALO == 0 and POOL_HALO == BF16_SUBLANES
    row = lambda w, col: pl.BlockSpec((tm, w), lambda i: (i, col))
    resident = lambda *shape: pl.BlockSpec((None,) + shape, lambda i: (layer,) + (0,) * len(shape),
                                           pipeline_mode=pl.Buffered(1))
    halo_blocks = tm // POOL_HALO
    gates = [row(c, _OFF[g] // c + half) for g in ("ga", "gb", "gc") for half in range(D_MODEL // c)]
    return pl.pallas_call(
        functools.partial(_mix_kernel, seq),
        out_shape=jax.ShapeDtypeStruct((m, D_MODEL), F32),
        grid=(m // tm,),
        in_specs=[row(D_MODEL, 0), row(c, 0), row(c, 0), row(c, _OFF["pu"] // c),
                  pl.BlockSpec((POOL_HALO, c), lambda i: (jnp.maximum(i * halo_blocks - 1, 0), _OFF["pu"] // c)),
                  *gates,
                  resident(POOL_GROUPS, POOL_GROUP_DIM, POOL_GROUP_DIM), resident(1, c),
                  resident(c, D_MODEL), resident(c, D_MODEL), resident(c, D_MODEL), resident(D_MODEL, D_MODEL),
                  resident(1, D_MODEL), resident(D_MODEL, D_FF), resident(D_MODEL, D_FF), resident(D_FF, D_MODEL)],
        out_specs=row(D_MODEL, 0),
        compiler_params=_params("parallel"),
        name="mix",
    )(x2d, ya, yc, proj2d, proj2d, *([proj2d] * len(gates)), pool_w, pool_scale,
      wa, wb, wc, wo, norm2_w, wg, wu, wd)


def _row_tile(m, want):
    t = min(m, want)
    assert m % t == 0
    return t


def _cast_kernel(*refs):
    n = len(refs) // 2
    for src, dst in zip(refs[:n], refs[n:]):
        dst[...] = src[...].astype(BF16)


def _to_bf16(*weights):
    flat = [w.reshape(-1, w.shape[-1]) for w in weights]
    assert all(w.shape[0] % (CAST_STEPS * BF16_SUBLANES) == 0 for w in flat)
    specs = [pl.BlockSpec((w.shape[0] // CAST_STEPS, w.shape[1]), lambda i: (i, 0)) for w in flat]
    out = pl.pallas_call(
        _cast_kernel,
        out_shape=[jax.ShapeDtypeStruct(w.shape, BF16) for w in flat],
        grid=(CAST_STEPS,),
        in_specs=specs,
        out_specs=specs,
        compiler_params=_params("parallel"),
        name="cast",
    )(*flat)
    return [o.reshape(w.shape) for o, w in zip(out, weights)]


def kernel(x, norm1_w, w_in, gla_wg2, gla_bg, gla_norm_w, pool_w, pool_scale, moba_qn_w, moba_kn_w,
           rel_bias, w_up_a, w_up_b, w_up_c, w_out, norm2_w, ffn_w_gate, ffn_w_up, ffn_w_down):
    b, s, d = x.shape
    assert d == D_MODEL and s % MOBA_BLOCK == 0 and s % GLA_T == 0
    m = b * s
    bias_tab = _bias_tables(rel_bias)
    w_in_bf, wa, wb, wc, wo, wg, wu, wd = _to_bf16(w_in, w_up_a, w_up_b, w_up_c, w_out, ffn_w_gate, ffn_w_up,
                                                   ffn_w_down)
    norm1 = norm1_w.reshape(DEPTH, 1, d)
    qn = moba_qn_w.reshape(DEPTH, 1, MOBA_HEAD_DIM)
    kn = moba_kn_w.reshape(DEPTH, 1, MOBA_HEAD_DIM)
    mix_weights = (pool_w.astype(BF16), pool_scale.reshape(DEPTH, 1, -1), wa, wb, wc, wo,
                   norm2_w.reshape(DEPTH, 1, d), wg, wu, wd)
    x2d = x.reshape(m, d)
    for l in range(DEPTH):
        proj2d, kmean = _inproj(x2d, norm1, w_in_bf, qn, kn, l, _row_tile(m, 512))
        proj = proj2d.reshape(b, s, PROJ_COLS)
        ya = _gla(proj, gla_wg2[l], gla_bg[l], gla_norm_w[l])
        yc = _moba(proj, kmean, rel_bias, bias_tab)
        x2d = _mix(x2d, ya.reshape(m, -1), yc.reshape(m, -1), proj2d, s, l, *mix_weights, _row_tile(s, 512))
    return x2d.reshape(b, s, d)
```

```python
import functools
import math

import numpy as np
import jax
import jax.numpy as jnp
from jax import lax
from jax.experimental import pallas as pl
from jax.experimental.pallas import tpu as pltpu

D_MODEL = 1024
DEPTH = 2
BRANCH_WIDTH = D_MODEL // 2
GLA_HEADS = 4
GLA_DV = BRANCH_WIDTH // GLA_HEADS
GLA_DK = GLA_DV // 2
GLA_LOW_RANK = 16
GLA_GATE_NORMALIZER = 16.0
POOL_WINDOWS = (2, 4, 8, 16)
POOL_GROUPS = 4
POOL_GROUP_DIM = BRANCH_WIDTH // POOL_GROUPS
MOBA_HEADS = 4
MOBA_HEAD_DIM = BRANCH_WIDTH // MOBA_HEADS
MOBA_BLOCK = 256
MOBA_TOPK = 3
REL_BUCKETS = 32
REL_MAX_EXACT = REL_BUCKETS // 2
REL_MAX_DIST = 128
D_FF = ((8 * D_MODEL + 767) // 768) * 256
NORM_EPS = 1e-6
NEG_INF = -1e30
LOG2E = math.log2(math.e)

LANES = 128
BF16_SUBLANES = 16
VMEM_LIMIT_BYTES = 56 * 1024 * 1024

_SECTIONS = (("gq", 256), ("gk", 256), ("gv", 512), ("glr", GLA_LOW_RANK), ("gr", 512), ("pu", 512),
             ("mq", 512), ("mk", 512), ("mv", 512), ("ga", 1024), ("gb", 1024), ("gc", 1024))
PROJ_CHUNK = 512
_OFF = {}
_o = 0
for _n, _w in _SECTIONS:
    if _n != "glr":
        assert _o % min(_w, PROJ_CHUNK) == 0
        _OFF[_n] = _o
        _o += _w
_OFF["glr"] = _o
PROJ_COLS = _o + LANES
GLR_SRC = sum(w for n, w in _SECTIONS[:3])
ROW_TILE = 512

GLA_T = 128
GLA_LEVELS = 7
GLA_BATCH_PER_STEP = 4
MOBA_BATCH_PER_STEP = 4
POOL_HALO = 16
FFN_CHUNK = 256

F32 = jnp.float32
BF16 = jnp.bfloat16


def _params(*sem):
    return pltpu.CompilerParams(dimension_semantics=sem, vmem_limit_bytes=VMEM_LIMIT_BYTES)


def _dot(a, b):
    return jnp.dot(a, b, preferred_element_type=F32)


def _dot_nt(a, b):
    return lax.dot_general(a, b, (((1,), (1,)), ((), ())), preferred_element_type=F32)


def _split_hi_lo(x):
    hi = x.astype(BF16)
    lo = (x - hi.astype(F32)).astype(BF16)
    return hi, lo


def _rms(x, w):
    return x * lax.rsqrt(jnp.mean(x * x, axis=-1, keepdims=True) + NORM_EPS) * w


def _sigmoid(x):
    return 1.0 / (1.0 + jnp.exp(-x))


def _head_rms(y, w, scale=1.0):
    dh = MOBA_HEAD_DIM
    return jnp.concatenate([_rms(y[:, h * dh:(h + 1) * dh], w) * scale for h in range(y.shape[1] // dh)], axis=1)


def _inproj_kernel(x_ref, nw_ref, w_ref, qn_ref, kn_ref, o_ref, km_ref, wp_ref):
    @pl.when(pl.program_id(0) == 0)
    def _():
        glr_end = GLR_SRC + GLA_LOW_RANK
        wp_ref[:, :GLR_SRC] = w_ref[:, :GLR_SRC]
        wp_ref[:, GLR_SRC:_OFF["glr"]] = w_ref[:, glr_end:]
        wp_ref[:, _OFF["glr"]:] = jnp.concatenate(
            [w_ref[:, GLR_SRC:glr_end], jnp.zeros((D_MODEL, LANES - GLA_LOW_RANK), BF16)], axis=1)

    h = _rms(x_ref[...], nw_ref[...]).astype(BF16)
    for lo in range(0, PROJ_COLS, PROJ_CHUNK):
        cols = slice(lo, min(lo + PROJ_CHUNK, PROJ_COLS))
        y = _dot(h, wp_ref[:, cols])
        if lo == _OFF["gr"]:
            y = y * _sigmoid(y)
        elif lo == _OFF["mq"]:
            y = _head_rms(y, qn_ref[...], MOBA_HEAD_DIM ** -0.5 * LOG2E)
        elif lo == _OFF["mk"]:
            y = _head_rms(y, kn_ref[...])
            for j in range(y.shape[0] // MOBA_BLOCK):
                km_ref[j] = jnp.mean(y[j * MOBA_BLOCK:(j + 1) * MOBA_BLOCK], axis=0, keepdims=True)
        o_ref[:, cols] = y.astype(BF16)


def _inproj(x2d, norm_w, w_bf, qn_w, kn_w, layer, tm):
    m = x2d.shape[0]
    in_cols = w_bf.shape[-1]
    assert tm % MOBA_BLOCK == 0 and all(_OFF[n] % PROJ_CHUNK == 0 for n in ("gr", "mq", "mk"))
    layer_block = lambda *shape: pl.BlockSpec((None,) + shape, lambda i: (layer,) + (0,) * len(shape))
    return pl.pallas_call(
        _inproj_kernel,
        out_shape=(jax.ShapeDtypeStruct((m, PROJ_COLS), BF16),
                   jax.ShapeDtypeStruct((m // MOBA_BLOCK, 1, BRANCH_WIDTH), F32)),
        grid=(m // tm,),
        in_specs=[pl.BlockSpec((tm, D_MODEL), lambda i: (i, 0)),
                  layer_block(1, D_MODEL),
                  pl.BlockSpec((None, D_MODEL, in_cols), lambda i: (layer, 0, 0), pipeline_mode=pl.Buffered(1)),
                  layer_block(1, MOBA_HEAD_DIM), layer_block(1, MOBA_HEAD_DIM)],
        out_specs=(pl.BlockSpec((tm, PROJ_COLS), lambda i: (i, 0)),
                   pl.BlockSpec((tm // MOBA_BLOCK, 1, BRANCH_WIDTH), lambda i: (i, 0, 0))),
        scratch_shapes=[pltpu.VMEM((D_MODEL, PROJ_COLS), BF16)],
        compiler_params=_params("arbitrary"),
        name="inproj",
    )(x2d, norm_w, w_bf, qn_w, kn_w)


def _gla_constants():
    t = np.arange(GLA_T)
    u = np.arange(GLA_T)
    a_rows = []
    level = np.full((GLA_T, GLA_T), -1, np.int32)
    for lev in range(GLA_LEVELS):
        s = (GLA_T // 2) >> lev
        g = 2 * s
        first = (t % g) < s
        ref = (t // g) * g + s - 1
        a = np.where(first[:, None],
                     (u[None, :] > t[:, None]) & (u[None, :] <= ref[:, None]),
                     (u[None, :] > ref[:, None]) & (u[None, :] <= t[:, None]))
        a_rows.append(a)
        same = (t[:, None] // g) == (t[None, :] // g)
        level[same & (~first)[:, None] & first[None, :]] = lev
    a_rows.append(u[None, :] <= t[:, None])
    level[np.eye(GLA_T, dtype=bool)] = GLA_LEVELS
    a_all = np.concatenate(a_rows, axis=0).astype(np.float32)
    akt = np.concatenate([(u[:, None] > t[None, :]), np.ones((GLA_T, GLA_T), bool)], axis=1).astype(np.float32)
    return np.concatenate([a_all, a_all], axis=1), np.concatenate([akt, akt], axis=0), level


def _log_sigmoid(z):
    return jnp.minimum(z, 0.0) - jnp.log1p(jnp.exp(-jnp.abs(z)))


def _gla_kernel(q_ref, k_ref, v_ref, sr_ref, glr_ref, wg2_ref, bg_ref, a_ref, akt_ref, lvl_ref, nw_ref, o_ref,
                s_ref, sbd_ref):
    nh, dk, dv, t = GLA_HEADS, GLA_DK, GLA_DV, GLA_T

    @pl.when(pl.program_id(1) == 0)
    def _():
        s_ref[...] = jnp.zeros_like(s_ref)
        sbd_ref[...] = jnp.zeros_like(sbd_ref)

    rows = range(q_ref.shape[0])
    each = lambda f: [f(bi) for bi in rows]
    q8 = each(lambda bi: q_ref[bi] * (dk ** -0.5))
    k = each(lambda bi: k_ref[bi])
    v = each(lambda bi: v_ref[bi])
    s_old = each(lambda bi: [s_ref[bi, h] for h in range(nh)])

    inv = 1.0 / GLA_GATE_NORMALIZER
    glog = each(lambda bi: _log_sigmoid(_dot(glr_ref[bi], wg2_ref[...]) + bg_ref[...]) * inv)
    glog_t = each(lambda bi: glog[bi].T)

    dec_all = each(lambda bi: jnp.exp(_dot(a_ref[...], jnp.concatenate(_split_hi_lo(glog[bi]), axis=0))).astype(BF16))
    et_all = each(lambda bi: _dot(jnp.concatenate(_split_hi_lo(glog_t[bi]), axis=1), akt_ref[...]))

    qe = each(lambda bi: q8[bi] * dec_all[bi][GLA_LEVELS * t:(GLA_LEVELS + 1) * t])
    groups = range(nh * dk // LANES)
    o_inter = each(lambda bi: jnp.concatenate(
        [_dot(qe[bi][:, g * LANES:(g + 1) * LANES], sbd_ref[bi, g]) for g in groups], axis=1))

    per_group = LANES // dk
    lane_head = lax.broadcasted_iota(jnp.int32, (t, LANES), 1) // dk
    head_masks = [lane_head == h for h in range(per_group)]
    lvl = lvl_ref[...]
    attn = each(lambda bi: [jnp.zeros((t, t), F32) for _ in range(nh)])
    for lev in range(GLA_LEVELS + 1):
        m = lvl == lev
        for bi in rows:
            if lev < GLA_LEVELS:
                e = dec_all[bi][lev * t:(lev + 1) * t]
                ql = q8[bi] * e
                kl = k[bi] * e
            else:
                ql = q8[bi]
                kl = k[bi]
            for g in range(nh // per_group):
                lanes = slice(g * LANES, (g + 1) * LANES)
                klg = kl[:, lanes]
                kbd = jnp.concatenate([jnp.where(hm, klg, jnp.zeros_like(klg)) for hm in head_masks], axis=0)
                sc = _dot_nt(ql[:, lanes], kbd)
                for j in range(per_group):
                    h = g * per_group + j
                    attn[bi][h] = jnp.where(m, sc[:, j * t:(j + 1) * t], attn[bi][h])

    nw = nw_ref[...]
    for bi in rows:
        outs = []
        for h in range(nh):
            cols = slice(h * dv, (h + 1) * dv)
            outs.append(_rms(_dot(attn[bi][h].astype(BF16), v[bi][:, cols]) + o_inter[bi][:, cols], nw))
        o_ref[bi] = (jnp.concatenate(outs, axis=1) * sr_ref[bi].astype(F32)).astype(BF16)

    k_tail_t = each(lambda bi: (k[bi].astype(F32).T * jnp.exp(et_all[bi][:, :t])).astype(BF16))
    dec_t = each(lambda bi: jnp.exp(et_all[bi][:, t:]))
    for bi in rows:
        for h in range(nh):
            ch = slice(h * dk, (h + 1) * dk)
            s_new = dec_t[bi][ch, :] * s_old[bi][h] + _dot(k_tail_t[bi][ch, :], v[bi][:, h * dv:(h + 1) * dv])
            s_ref[bi, h] = s_new
            j = h % (LANES // dk)
            sbd_ref[bi, h * dk // LANES, j * dk:(j + 1) * dk, j * dv:(j + 1) * dv] = s_new.astype(BF16)


def _gla(proj, wg2, bg, norm_w):
    b, s, _ = proj.shape
    t = GLA_T
    assert t == GLA_DV
    hk = GLA_HEADS * GLA_DK
    hv = GLA_HEADS * GLA_DV
    a_all, akt, level = _gla_constants()
    wg2_p = jnp.zeros((LANES, hk), F32).at[:GLA_LOW_RANK].set(wg2).astype(BF16)
    const = lambda *shape: pl.BlockSpec(shape, lambda i, j: (0,) * len(shape))
    nb = GLA_BATCH_PER_STEP if b % GLA_BATCH_PER_STEP == 0 else 1
    return pl.pallas_call(
        _gla_kernel,
        out_shape=jax.ShapeDtypeStruct((b, s, hv), BF16),
        grid=(b // nb, s // t),
        in_specs=[pl.BlockSpec((nb, t, hk), lambda i, j: (i, j, _OFF["gq"] // hk)),
                  pl.BlockSpec((nb, t, hk), lambda i, j: (i, j, _OFF["gk"] // hk)),
                  pl.BlockSpec((nb, t, hv), lambda i, j: (i, j, _OFF["gv"] // hv)),
                  pl.BlockSpec((nb, t, hv), lambda i, j: (i, j, _OFF["gr"] // hv)),
                  pl.BlockSpec((nb, t, LANES), lambda i, j: (i, j, _OFF["glr"] // LANES)),
                  const(LANES, hk), const(1, hk),
                  const((GLA_LEVELS + 1) * t, 2 * t), const(2 * t, 2 * t), const(t, t),
                  const(1, GLA_DV)],
        out_specs=pl.BlockSpec((nb, t, hv), lambda i, j: (i, j, 0)),
        scratch_shapes=[pltpu.VMEM((nb, GLA_HEADS, GLA_DK, GLA_DV), F32),
                        pltpu.VMEM((nb, hk // LANES, LANES, LANES // GLA_DK * GLA_DV), BF16)],
        compiler_params=_params("parallel", "arbitrary"),
        name="gla",
    )(proj, proj, proj, proj, proj, wg2_p, bg.reshape(1, hk),
      jnp.asarray(a_all, BF16), jnp.asarray(akt, BF16), jnp.asarray(level), norm_w.reshape(1, GLA_DV))


def _t5_bucket_np(dist):
    n = np.maximum(dist, 0)
    large = REL_MAX_EXACT + (np.log(np.maximum(n, 1).astype(np.float32) / REL_MAX_EXACT)
                             / math.log(REL_MAX_DIST / REL_MAX_EXACT)
                             * (REL_BUCKETS - REL_MAX_EXACT)).astype(np.int32)
    large = np.minimum(large, REL_BUCKETS - 1)
    return np.where(n < REL_MAX_EXACT, n, large).astype(np.int32)


def _bias_kernel(rb_ref, bk_ref, o_ref):
    h = pl.program_id(0)
    for which in range(2):
        bk = bk_ref[which]
        acc = jnp.full(bk.shape, NEG_INF, F32)
        for bucket in range(REL_BUCKETS):
            acc = jnp.where(bk == bucket, rb_ref[bucket, h] * LOG2E, acc)
        o_ref[0, which] = acc


def _bias_tables(rel_bias):
    blk = MOBA_BLOCK
    j = np.arange(blk)[:, None]
    i = np.arange(blk)[None, :]
    own = np.where(i >= j, _t5_bucket_np(i - j), -1)
    prev = _t5_bucket_np(i + blk - j)
    buckets = np.stack([own, prev]).astype(np.int32)
    return pl.pallas_call(
        _bias_kernel,
        out_shape=jax.ShapeDtypeStruct((MOBA_HEADS, 2, blk, blk), F32),
        grid=(MOBA_HEADS,),
        in_specs=[pl.BlockSpec(memory_space=pltpu.SMEM),
                  pl.BlockSpec((2, blk, blk), lambda h: (0, 0, 0))],
        out_specs=pl.BlockSpec((1, 2, blk, blk), lambda h: (h, 0, 0, 0)),
        compiler_params=_params("parallel"),
        name="moba_bias",
    )(rel_bias, jnp.asarray(buckets))


def _moba_kernel(rb_ref, q_ref, k_ref, v_ref, km_ref, bias_ref, o_ref, ks_ref, vt_ref, sel_ref):
    blk, dh, nh = MOBA_BLOCK, MOBA_HEAD_DIM, MOBA_HEADS
    nb = ks_ref.shape[1] // blk
    qb = pl.program_id(1)
    chains = [(bi, h) for bi in range(q_ref.shape[0]) for h in range(nh)]
    each = lambda f: [f(c, bi, h) for c, (bi, h) in enumerate(chains)]
    head = lambda h: slice(h * dh, (h + 1) * dh)
    keys = lambda bi, h, n: ks_ref[bi, pl.ds(pl.multiple_of(n * blk, blk), blk), head(h)]

    ones = jnp.ones((BF16_SUBLANES, blk), BF16)
    for bi in range(q_ref.shape[0]):
        ks_ref[bi, pl.ds(pl.multiple_of(qb * blk, blk), blk), :] = k_ref[bi]
    for c, (bi, h) in enumerate(chains):
        vt_ref[c, qb] = jnp.concatenate([v_ref[bi, :, head(h)].astype(F32).T.astype(BF16), ones], axis=0)

    n_iota = lax.broadcasted_iota(jnp.int32, (nb, blk), 0)
    q_sc = each(lambda c, bi, h: q_ref[bi, :, head(h)])

    s_own = each(lambda c, bi, h: _dot_nt(keys(bi, h, qb), q_sc[c]) + bias_ref[h, 0])
    m0 = each(lambda c, bi, h: jnp.max(s_own[c], axis=0, keepdims=True))
    p0 = each(lambda c, bi, h: jnp.exp2(s_own[c] - m0[c]))
    carry = each(lambda c, bi, h: (m0[c], _dot(vt_ref[c, qb], p0[c].astype(BF16))))

    for c, (bi, h) in enumerate(chains):
        km_hi, km_lo = _split_hi_lo(km_ref[bi * nb:(bi + 1) * nb, 0, head(h)])
        sc = _dot_nt(km_hi, q_sc[c]) + _dot_nt(km_lo, q_sc[c])
        rank = jnp.zeros((nb, blk), jnp.int32)
        for m in range(nb):
            sm = sc[m:m + 1, :]
            beats = ((sm > sc) | ((sm == sc) & (m < n_iota))) & (m < qb)
            rank = rank + beats.astype(jnp.int32)
        sel_ref[c] = jnp.where((n_iota < qb) & (rank < MOBA_TOPK), 0.0, NEG_INF)

    def attend(n, tile_bias, row_bias, carry):
        s = each(lambda c, bi, h: _dot_nt(keys(bi, h, n), q_sc[c]))
        if tile_bias is not None:
            s = each(lambda c, bi, h: s[c] + tile_bias(c, h))
        rowb = each(lambda c, bi, h: row_bias(c, h))
        m_new = each(lambda c, bi, h: jnp.maximum(carry[c][0], jnp.max(s[c], axis=0, keepdims=True) + rowb[c]))
        alpha = each(lambda c, bi, h: jnp.exp2(carry[c][0] - m_new[c]))
        p = each(lambda c, bi, h: jnp.exp2(s[c] - (m_new[c] - rowb[c])))
        pv = each(lambda c, bi, h: _dot(vt_ref[c, n], p[c].astype(BF16)))
        return each(lambda c, bi, h: (m_new[c], alpha[c] * carry[c][1] + pv[c]))

    prev = jnp.maximum(qb - 1, 0)
    carry = attend(prev, lambda c, h: bias_ref[h, 1], lambda c, h: sel_ref[c, pl.ds(prev, 1), :], carry)
    carry = lax.fori_loop(
        0, prev,
        lambda n, cr: attend(n, None, lambda c, h: sel_ref[c, pl.ds(n, 1), :] + rb_ref[REL_BUCKETS - 1, h] * LOG2E,
                             cr), carry)
    outs = each(lambda c, bi, h: (carry[c][1][:dh] / carry[c][1][dh:dh + 1]).T)
    for bi in range(q_ref.shape[0]):
        o_ref[bi] = jnp.concatenate(outs[bi * nh:(bi + 1) * nh], axis=1).astype(BF16)


def _moba(proj, kmean, rel_bias, bias_tab):
    b, s, _ = proj.shape
    blk, dh, nh = MOBA_BLOCK, MOBA_HEAD_DIM, MOBA_HEADS
    c = nh * dh
    nb = s // blk
    rows = MOBA_BATCH_PER_STEP if b % MOBA_BATCH_PER_STEP == 0 else 1
    nc = rows * nh
    return pl.pallas_call(
        _moba_kernel,
        out_shape=jax.ShapeDtypeStruct((b, s, c), BF16),
        grid=(b // rows, nb),
        in_specs=[pl.BlockSpec(memory_space=pltpu.SMEM),
                  pl.BlockSpec((rows, blk, c), lambda i, j: (i, j, _OFF["mq"] // c)),
                  pl.BlockSpec((rows, blk, c), lambda i, j: (i, j, _OFF["mk"] // c)),
                  pl.BlockSpec((rows, blk, c), lambda i, j: (i, j, _OFF["mv"] // c)),
                  pl.BlockSpec((rows * nb, 1, c), lambda i, j: (i, 0, 0)),
                  pl.BlockSpec((nh, 2, blk, blk), lambda i, j: (0, 0, 0, 0))],
        out_specs=pl.BlockSpec((rows, blk, c), lambda i, j: (i, j, 0)),
        scratch_shapes=[pltpu.VMEM((rows, s, c), BF16), pltpu.VMEM((nc, nb, dh + BF16_SUBLANES, blk), BF16),
                        pltpu.VMEM((nc, nb, blk), F32)],
        compiler_params=_params("parallel", "arbitrary"),
        name="moba",
    )(rel_bias, proj, proj, proj, kmean, bias_tab)


def _pool_branch(cur, halo, pw_ref, ps_ref, start):
    tm = cur.shape[0]
    ext = jnp.concatenate([halo, cur], axis=0)
    pos = start + lax.broadcasted_iota(jnp.int32, (tm, 1), 0)
    outs = []
    for g, w in enumerate(POOL_WINDOWS):
        cols = slice(g * POOL_GROUP_DIM, (g + 1) * POOL_GROUP_DIM)
        a = ext[:, cols]
        shift = 1
        while shift < w:
            a = a + pltpu.roll(a, shift, 0)
            shift *= 2
        cnt = jnp.minimum(pos + 1, w).astype(F32)
        p = a[POOL_HALO:, :] / cnt - cur[:, cols]
        outs.append(_dot(p.astype(BF16), pw_ref[g]))
    return jnp.concatenate(outs, axis=1) * ps_ref[...]


def _mix_kernel(seq, x_ref, ya_ref, yc_ref, u_ref, halo_ref, ga0_ref, ga1_ref, gb0_ref, gb1_ref, gc0_ref, gc1_ref,
                pw_ref, ps_ref, wa_ref, wb_ref, wc_ref, wo_ref, n2_ref, wg_ref, wu_ref, wd_ref, o_ref):
    tm = x_ref.shape[0]
    start = (pl.program_id(0) * tm) % seq
    halo = jnp.where(start > 0, halo_ref[...].astype(F32), 0.0)
    yb = _pool_branch(u_ref[...].astype(F32), halo, pw_ref, ps_ref, start)
    gate = lambda lo_ref, hi_ref: _sigmoid(jnp.concatenate([lo_ref[...], hi_ref[...]], axis=1).astype(F32))
    merged = (gate(ga0_ref, ga1_ref) * _dot(ya_ref[...], wa_ref[...])
              + gate(gb0_ref, gb1_ref) * _dot(yb.astype(BF16), wb_ref[...])
              + gate(gc0_ref, gc1_ref) * _dot(yc_ref[...], wc_ref[...]))
    x1 = x_ref[...] + _dot(merged.astype(BF16), wo_ref[...])
    h = _rms(x1, n2_ref[...]).astype(BF16)
    acc = x1
    for lo in range(0, D_FF, FFN_CHUNK):
        cols = slice(lo, min(lo + FFN_CHUNK, D_FF))
        g = _dot(h, wg_ref[:, cols])
        u = _dot(h, wu_ref[:, cols])
        act = (g * _sigmoid(g) * u).astype(BF16)
        acc = acc + _dot(act, wd_ref[cols, :])
    o_ref[...] = acc


def _mix(x2d, ya, yc, proj2d, seq, layer, pool_w, pool_scale, wa, wb, wc, wo, norm2_w, wg, wu, wd, tm):
    m = x2d.shape[0]
    c = BRANCH_WIDTH
    assert seq % tm == 0 and tm % POOL_HALO == 0 and POOL_HALO == BF16_SUBLANES
    row = lambda w, col: pl.BlockSpec((tm, w), lambda i: (i, col))
    resident = lambda *shape: pl.BlockSpec((None,) + shape, lambda i: (layer,) + (0,) * len(shape),
                                           pipeline_mode=pl.Buffered(1))
    halo_blocks = tm // POOL_HALO
    gates = [row(c, _OFF[g] // c + half) for g in ("ga", "gb", "gc") for half in range(D_MODEL // c)]
    return pl.pallas_call(
        functools.partial(_mix_kernel, seq),
        out_shape=jax.ShapeDtypeStruct((m, D_MODEL), F32),
        grid=(m // tm,),
        in_specs=[row(D_MODEL, 0), row(c, 0), row(c, 0), row(c, _OFF["pu"] // c),
                  pl.BlockSpec((POOL_HALO, c), lambda i: (jnp.maximum(i * halo_blocks - 1, 0), _OFF["pu"] // c)),
                  *gates,
                  resident(POOL_GROUPS, POOL_GROUP_DIM, POOL_GROUP_DIM), resident(1, c),
                  resident(c, D_MODEL), resident(c, D_MODEL), resident(c, D_MODEL), resident(D_MODEL, D_MODEL),
                  resident(1, D_MODEL), resident(D_MODEL, D_FF), resident(D_MODEL, D_FF), resident(D_FF, D_MODEL)],
        out_specs=row(D_MODEL, 0),
        compiler_params=_params("parallel"),
        name="mix",
    )(x2d, ya, yc, proj2d, proj2d, *([proj2d] * len(gates)), pool_w, pool_scale,
      wa, wb, wc, wo, norm2_w, wg, wu, wd)


def _row_tile(m, want):
    t = min(m, want)
    assert m % t == 0
    return t


def kernel(x, norm1_w, w_in, gla_wg2, gla_bg, gla_norm_w, pool_w, pool_scale, moba_qn_w, moba_kn_w,
           rel_bias, w_up_a, w_up_b, w_up_c, w_out, norm2_w, ffn_w_gate, ffn_w_up, ffn_w_down):
    b, s, d = x.shape
    assert d == D_MODEL and s % MOBA_BLOCK == 0 and s % GLA_T == 0
    m = b * s
    bf = lambda a: a.astype(BF16)
    bias_tab = _bias_tables(rel_bias)
    inproj_params = (norm1_w.reshape(DEPTH, 1, d), bf(w_in), moba_qn_w.reshape(DEPTH, 1, MOBA_HEAD_DIM),
                     moba_kn_w.reshape(DEPTH, 1, MOBA_HEAD_DIM))
    mix_weights = (bf(pool_w), pool_scale.reshape(DEPTH, 1, -1), bf(w_up_a), bf(w_up_b), bf(w_up_c), bf(w_out),
                   norm2_w.reshape(DEPTH, 1, d), bf(ffn_w_gate), bf(ffn_w_up), bf(ffn_w_down))
    tm = _row_tile(s, ROW_TILE)
    x2d = x.reshape(m, d)
    for l in range(DEPTH):
        proj2d, kmean = _inproj(x2d, *inproj_params, l, tm)
        proj = proj2d.reshape(b, s, PROJ_COLS)
        ya = _gla(proj, gla_wg2[l], gla_bg[l], gla_norm_w[l])
        yc = _moba(proj, kmean, rel_bias, bias_tab)
        x2d = _mix(x2d, ya.reshape(m, -1), yc.reshape(m, -1), proj2d, s, l, *mix_weights, tm)
    return x2d.reshape(b, s, d)
```

```python
import functools
import math

import numpy as np
import jax
import jax.numpy as jnp
from jax import lax
from jax.experimental import pallas as pl
from jax.experimental.pallas import tpu as pltpu

D_MODEL = 1024
DEPTH = 2
BRANCH_WIDTH = D_MODEL // 2
GLA_HEADS = 4
GLA_DV = BRANCH_WIDTH // GLA_HEADS
GLA_DK = GLA_DV // 2
GLA_LOW_RANK = 16
GLA_GATE_NORMALIZER = 16.0
POOL_WINDOWS = (2, 4, 8, 16)
POOL_GROUPS = 4
POOL_GROUP_DIM = BRANCH_WIDTH // POOL_GROUPS
MOBA_HEADS = 4
MOBA_HEAD_DIM = BRANCH_WIDTH // MOBA_HEADS
MOBA_BLOCK = 256
MOBA_TOPK = 3
REL_BUCKETS = 32
REL_MAX_EXACT = REL_BUCKETS // 2
REL_MAX_DIST = 128
D_FF = ((8 * D_MODEL + 767) // 768) * 256
NORM_EPS = 1e-6
NEG_INF = -1e30
LOG2E = math.log2(math.e)

LANES = 128
BF16_SUBLANES = 16
VMEM_LIMIT_BYTES = 56 * 1024 * 1024

_SECTIONS = (("gq", 256), ("gk", 256), ("gv", 512), ("glr", GLA_LOW_RANK), ("gr", 512), ("pu", 512),
             ("mq", 512), ("mk", 512), ("mv", 512), ("ga", 1024), ("gb", 1024), ("gc", 1024))
PROJ_CHUNK = 512
_OFF = {}
_o = 0
for _n, _w in _SECTIONS:
    if _n != "glr":
        assert _o % min(_w, PROJ_CHUNK) == 0
        _OFF[_n] = _o
        _o += _w
_OFF["glr"] = _o
PROJ_COLS = _o + LANES
GLR_SRC = sum(w for n, w in _SECTIONS[:3])
ROW_TILE = 512

GLA_T = 128
GLA_LEVELS = 7
GLA_BATCH_PER_STEP = 8
MOBA_BATCH_PER_STEP = 4
POOL_HALO = 16
FFN_CHUNK = 256

F32 = jnp.float32
BF16 = jnp.bfloat16


def _params(*sem):
    return pltpu.CompilerParams(dimension_semantics=sem, vmem_limit_bytes=VMEM_LIMIT_BYTES)


def _dot(a, b):
    return jnp.dot(a, b, preferred_element_type=F32)


def _dot_nt(a, b):
    return lax.dot_general(a, b, (((1,), (1,)), ((), ())), preferred_element_type=F32)


def _split_hi_lo(x):
    hi = x.astype(BF16)
    lo = (x - hi.astype(F32)).astype(BF16)
    return hi, lo


def _rms(x, w):
    return x * lax.rsqrt(jnp.mean(x * x, axis=-1, keepdims=True) + NORM_EPS) * w


def _sigmoid(x):
    return 1.0 / (1.0 + jnp.exp(-x))


def _head_rms(y, w, scale=1.0):
    dh = MOBA_HEAD_DIM
    return jnp.concatenate([_rms(y[:, h * dh:(h + 1) * dh], w) * scale for h in range(y.shape[1] // dh)], axis=1)


def _inproj_kernel(x_ref, nw_ref, w_ref, qn_ref, kn_ref, o_ref, km_ref, wp_ref):
    @pl.when(pl.program_id(0) == 0)
    def _():
        glr_end = GLR_SRC + GLA_LOW_RANK
        wp_ref[:, :GLR_SRC] = w_ref[:, :GLR_SRC]
        wp_ref[:, GLR_SRC:_OFF["glr"]] = w_ref[:, glr_end:]
        wp_ref[:, _OFF["glr"]:] = jnp.concatenate(
            [w_ref[:, GLR_SRC:glr_end], jnp.zeros((D_MODEL, LANES - GLA_LOW_RANK), BF16)], axis=1)

    h = _rms(x_ref[...], nw_ref[...]).astype(BF16)
    for lo in range(0, PROJ_COLS, PROJ_CHUNK):
        cols = slice(lo, min(lo + PROJ_CHUNK, PROJ_COLS))
        y = _dot(h, wp_ref[:, cols])
        if lo == _OFF["gr"]:
            y = y * _sigmoid(y)
        elif lo == _OFF["mq"]:
            y = _head_rms(y, qn_ref[...], MOBA_HEAD_DIM ** -0.5 * LOG2E)
        elif lo == _OFF["mk"]:
            y = _head_rms(y, kn_ref[...])
            for j in range(y.shape[0] // MOBA_BLOCK):
                km_ref[j] = jnp.mean(y[j * MOBA_BLOCK:(j + 1) * MOBA_BLOCK], axis=0, keepdims=True)
        o_ref[:, cols] = y.astype(BF16)


def _inproj(x2d, norm_w, w_bf, qn_w, kn_w, layer, tm):
    m = x2d.shape[0]
    in_cols = w_bf.shape[-1]
    assert tm % MOBA_BLOCK == 0 and all(_OFF[n] % PROJ_CHUNK == 0 for n in ("gr", "mq", "mk"))
    layer_block = lambda *shape: pl.BlockSpec((None,) + shape, lambda i: (layer,) + (0,) * len(shape))
    return pl.pallas_call(
        _inproj_kernel,
        out_shape=(jax.ShapeDtypeStruct((m, PROJ_COLS), BF16),
                   jax.ShapeDtypeStruct((m // MOBA_BLOCK, 1, BRANCH_WIDTH), F32)),
        grid=(m // tm,),
        in_specs=[pl.BlockSpec((tm, D_MODEL), lambda i: (i, 0)),
                  layer_block(1, D_MODEL),
                  pl.BlockSpec((None, D_MODEL, in_cols), lambda i: (layer, 0, 0), pipeline_mode=pl.Buffered(1)),
                  layer_block(1, MOBA_HEAD_DIM), layer_block(1, MOBA_HEAD_DIM)],
        out_specs=(pl.BlockSpec((tm, PROJ_COLS), lambda i: (i, 0)),
                   pl.BlockSpec((tm // MOBA_BLOCK, 1, BRANCH_WIDTH), lambda i: (i, 0, 0))),
        scratch_shapes=[pltpu.VMEM((D_MODEL, PROJ_COLS), BF16)],
        compiler_params=_params("arbitrary"),
        name="inproj",
    )(x2d, norm_w, w_bf, qn_w, kn_w)


def _gla_constants():
    t = np.arange(GLA_T)
    u = np.arange(GLA_T)
    a_rows = []
    level = np.full((GLA_T, GLA_T), -1, np.int32)
    for lev in range(GLA_LEVELS):
        s = (GLA_T // 2) >> lev
        g = 2 * s
        first = (t % g) < s
        ref = (t // g) * g + s - 1
        a = np.where(first[:, None],
                     (u[None, :] > t[:, None]) & (u[None, :] <= ref[:, None]),
                     (u[None, :] > ref[:, None]) & (u[None, :] <= t[:, None]))
        a_rows.append(a)
        same = (t[:, None] // g) == (t[None, :] // g)
        level[same & (~first)[:, None] & first[None, :]] = lev
    a_rows.append(u[None, :] <= t[:, None])
    level[np.eye(GLA_T, dtype=bool)] = GLA_LEVELS
    a_all = np.concatenate(a_rows, axis=0).astype(np.float32)
    akt = np.concatenate([(u[:, None] > t[None, :]), np.ones((GLA_T, GLA_T), bool)], axis=1).astype(np.float32)
    return np.concatenate([a_all, a_all], axis=1), np.concatenate([akt, akt], axis=0), level


def _log_sigmoid(z):
    return jnp.minimum(z, 0.0) - jnp.log1p(jnp.exp(-jnp.abs(z)))


def _gla_kernel(q_ref, k_ref, v_ref, sr_ref, glr_ref, wg2_ref, bg_ref, a_ref, akt_ref, lvl_ref, nw_ref, o_ref,
                s_ref, sbd_ref):
    nh, dk, dv, t = GLA_HEADS, GLA_DK, GLA_DV, GLA_T

    @pl.when(pl.program_id(1) == 0)
    def _():
        s_ref[...] = jnp.zeros_like(s_ref)
        sbd_ref[...] = jnp.zeros_like(sbd_ref)

    rows = range(q_ref.shape[0])
    each = lambda f: [f(bi) for bi in rows]
    q8 = each(lambda bi: q_ref[bi] * (dk ** -0.5))
    k = each(lambda bi: k_ref[bi])
    v = each(lambda bi: v_ref[bi])
    s_old = each(lambda bi: [s_ref[bi, h] for h in range(nh)])

    inv = 1.0 / GLA_GATE_NORMALIZER
    glog = each(lambda bi: _log_sigmoid(_dot(glr_ref[bi], wg2_ref[...]) + bg_ref[...]) * inv)
    glog_t = each(lambda bi: glog[bi].T)

    dec_all = each(lambda bi: jnp.exp(_dot(a_ref[...], jnp.concatenate(_split_hi_lo(glog[bi]), axis=0))).astype(BF16))
    et_all = each(lambda bi: _dot(jnp.concatenate(_split_hi_lo(glog_t[bi]), axis=1), akt_ref[...]))

    qe = each(lambda bi: q8[bi] * dec_all[bi][GLA_LEVELS * t:(GLA_LEVELS + 1) * t])
    groups = range(nh * dk // LANES)
    o_inter = each(lambda bi: jnp.concatenate(
        [_dot(qe[bi][:, g * LANES:(g + 1) * LANES], sbd_ref[bi, g]) for g in groups], axis=1))

    per_group = LANES // dk
    lane_head = lax.broadcasted_iota(jnp.int32, (t, LANES), 1) // dk
    head_masks = [lane_head == h for h in range(per_group)]
    lvl = lvl_ref[...]
    attn = each(lambda bi: [jnp.zeros((t, t), F32) for _ in range(nh)])
    for lev in range(GLA_LEVELS + 1):
        m = lvl == lev
        for bi in rows:
            if lev < GLA_LEVELS:
                e = dec_all[bi][lev * t:(lev + 1) * t]
                ql = q8[bi] * e
                kl = k[bi] * e
            else:
                ql = q8[bi]
                kl = k[bi]
            for g in range(nh // per_group):
                lanes = slice(g * LANES, (g + 1) * LANES)
                klg = kl[:, lanes]
                kbd = jnp.concatenate([jnp.where(hm, klg, jnp.zeros_like(klg)) for hm in head_masks], axis=0)
                sc = _dot_nt(ql[:, lanes], kbd)
                for j in range(per_group):
                    h = g * per_group + j
                    attn[bi][h] = jnp.where(m, sc[:, j * t:(j + 1) * t], attn[bi][h])

    nw = nw_ref[...]
    for bi in rows:
        outs = []
        for h in range(nh):
            cols = slice(h * dv, (h + 1) * dv)
            outs.append(_rms(_dot(attn[bi][h].astype(BF16), v[bi][:, cols]) + o_inter[bi][:, cols], nw))
        o_ref[bi] = (jnp.concatenate(outs, axis=1) * sr_ref[bi].astype(F32)).astype(BF16)

    k_tail_t = each(lambda bi: (k[bi].astype(F32).T * jnp.exp(et_all[bi][:, :t])).astype(BF16))
    dec_t = each(lambda bi: jnp.exp(et_all[bi][:, t:]))
    for bi in rows:
        for h in range(nh):
            ch = slice(h * dk, (h + 1) * dk)
            s_new = dec_t[bi][ch, :] * s_old[bi][h] + _dot(k_tail_t[bi][ch, :], v[bi][:, h * dv:(h + 1) * dv])
            s_ref[bi, h] = s_new
            j = h % (LANES // dk)
            sbd_ref[bi, h * dk // LANES, j * dk:(j + 1) * dk, j * dv:(j + 1) * dv] = s_new.astype(BF16)


def _gla(proj, wg2, bg, norm_w):
    b, s, _ = proj.shape
    t = GLA_T
    assert t == GLA_DV
    hk = GLA_HEADS * GLA_DK
    hv = GLA_HEADS * GLA_DV
    a_all, akt, level = _gla_constants()
    wg2_p = jnp.zeros((LANES, hk), F32).at[:GLA_LOW_RANK].set(wg2).astype(BF16)
    const = lambda *shape: pl.BlockSpec(shape, lambda i, j: (0,) * len(shape))
    nb = GLA_BATCH_PER_STEP if b % GLA_BATCH_PER_STEP == 0 else 1
    return pl.pallas_call(
        _gla_kernel,
        out_shape=jax.ShapeDtypeStruct((b, s, hv), BF16),
        grid=(b // nb, s // t),
        in_specs=[pl.BlockSpec((nb, t, hk), lambda i, j: (i, j, _OFF["gq"] // hk)),
                  pl.BlockSpec((nb, t, hk), lambda i, j: (i, j, _OFF["gk"] // hk)),
                  pl.BlockSpec((nb, t, hv), lambda i, j: (i, j, _OFF["gv"] // hv)),
                  pl.BlockSpec((nb, t, hv), lambda i, j: (i, j, _OFF["gr"] // hv)),
                  pl.BlockSpec((nb, t, LANES), lambda i, j: (i, j, _OFF["glr"] // LANES)),
                  const(LANES, hk), const(1, hk),
                  const((GLA_LEVELS + 1) * t, 2 * t), const(2 * t, 2 * t), const(t, t),
                  const(1, GLA_DV)],
        out_specs=pl.BlockSpec((nb, t, hv), lambda i, j: (i, j, 0)),
        scratch_shapes=[pltpu.VMEM((nb, GLA_HEADS, GLA_DK, GLA_DV), F32),
                        pltpu.VMEM((nb, hk // LANES, LANES, LANES // GLA_DK * GLA_DV), BF16)],
        compiler_params=_params("parallel", "arbitrary"),
        name="gla",
    )(proj, proj, proj, proj, proj, wg2_p, bg.reshape(1, hk),
      jnp.asarray(a_all, BF16), jnp.asarray(akt, BF16), jnp.asarray(level), norm_w.reshape(1, GLA_DV))


def _t5_bucket_np(dist):
    n = np.maximum(dist, 0)
    large = REL_MAX_EXACT + (np.log(np.maximum(n, 1).astype(np.float32) / REL_MAX_EXACT)
                             / math.log(REL_MAX_DIST / REL_MAX_EXACT)
                             * (REL_BUCKETS - REL_MAX_EXACT)).astype(np.int32)
    large = np.minimum(large, REL_BUCKETS - 1)
    return np.where(n < REL_MAX_EXACT, n, large).astype(np.int32)


def _bias_kernel(rb_ref, bk_ref, o_ref):
    h = pl.program_id(0)
    for which in range(2):
        bk = bk_ref[which]
        acc = jnp.full(bk.shape, NEG_INF, F32)
        for bucket in range(REL_BUCKETS):
            acc = jnp.where(bk == bucket, rb_ref[bucket, h] * LOG2E, acc)
        o_ref[0, which] = acc


def _bias_tables(rel_bias):
    blk = MOBA_BLOCK
    j = np.arange(blk)[:, None]
    i = np.arange(blk)[None, :]
    own = np.where(i >= j, _t5_bucket_np(i - j), -1)
    prev = _t5_bucket_np(i + blk - j)
    buckets = np.stack([own, prev]).astype(np.int32)
    return pl.pallas_call(
        _bias_kernel,
        out_shape=jax.ShapeDtypeStruct((MOBA_HEADS, 2, blk, blk), F32),
        grid=(MOBA_HEADS,),
        in_specs=[pl.BlockSpec(memory_space=pltpu.SMEM),
                  pl.BlockSpec((2, blk, blk), lambda h: (0, 0, 0))],
        out_specs=pl.BlockSpec((1, 2, blk, blk), lambda h: (h, 0, 0, 0)),
        compiler_params=_params("parallel"),
        name="moba_bias",
    )(rel_bias, jnp.asarray(buckets))


def _moba_kernel(rb_ref, q_ref, k_ref, v_ref, km_ref, bias_ref, o_ref, ks_ref, vt_ref, sel_ref):
    blk, dh, nh = MOBA_BLOCK, MOBA_HEAD_DIM, MOBA_HEADS
    nb = ks_ref.shape[1] // blk
    qb = pl.program_id(1)
    chains = [(bi, h) for bi in range(q_ref.shape[0]) for h in range(nh)]
    each = lambda f: [f(c, bi, h) for c, (bi, h) in enumerate(chains)]
    head = lambda h: slice(h * dh, (h + 1) * dh)
    keys = lambda bi, h, n: ks_ref[bi, pl.ds(pl.multiple_of(n * blk, blk), blk), head(h)]

    ones = jnp.ones((BF16_SUBLANES, blk), BF16)
    for bi in range(q_ref.shape[0]):
        ks_ref[bi, pl.ds(pl.multiple_of(qb * blk, blk), blk), :] = k_ref[bi]
    for c, (bi, h) in enumerate(chains):
        vt_ref[c, qb] = jnp.concatenate([v_ref[bi, :, head(h)].astype(F32).T.astype(BF16), ones], axis=0)

    n_iota = lax.broadcasted_iota(jnp.int32, (nb, blk), 0)
    q_sc = each(lambda c, bi, h: q_ref[bi, :, head(h)])

    s_own = each(lambda c, bi, h: _dot_nt(keys(bi, h, qb), q_sc[c]) + bias_ref[h, 0])
    m0 = each(lambda c, bi, h: jnp.max(s_own[c], axis=0, keepdims=True))
    p0 = each(lambda c, bi, h: jnp.exp2(s_own[c] - m0[c]))
    carry = each(lambda c, bi, h: (m0[c], _dot(vt_ref[c, qb], p0[c].astype(BF16))))

    for c, (bi, h) in enumerate(chains):
        km_hi, km_lo = _split_hi_lo(km_ref[bi * nb:(bi + 1) * nb, 0, head(h)])
        sc = _dot_nt(km_hi, q_sc[c]) + _dot_nt(km_lo, q_sc[c])
        rank = jnp.zeros((nb, blk), jnp.int32)
        for m in range(nb):
            sm = sc[m:m + 1, :]
            beats = ((sm > sc) | ((sm == sc) & (m < n_iota))) & (m < qb)
            rank = rank + beats.astype(jnp.int32)
        sel_ref[c] = jnp.where((n_iota < qb) & (rank < MOBA_TOPK), 0.0, NEG_INF)

    def attend(n, tile_bias, row_bias, carry):
        s = each(lambda c, bi, h: _dot_nt(keys(bi, h, n), q_sc[c]))
        if tile_bias is not None:
            s = each(lambda c, bi, h: s[c] + tile_bias(c, h))
        rowb = each(lambda c, bi, h: row_bias(c, h))
        m_new = each(lambda c, bi, h: jnp.maximum(carry[c][0], jnp.max(s[c], axis=0, keepdims=True) + rowb[c]))
        alpha = each(lambda c, bi, h: jnp.exp2(carry[c][0] - m_new[c]))
        p = each(lambda c, bi, h: jnp.exp2(s[c] - (m_new[c] - rowb[c])))
        pv = each(lambda c, bi, h: _dot(vt_ref[c, n], p[c].astype(BF16)))
        return each(lambda c, bi, h: (m_new[c], alpha[c] * carry[c][1] + pv[c]))

    prev = jnp.maximum(qb - 1, 0)
    carry = attend(prev, lambda c, h: bias_ref[h, 1], lambda c, h: sel_ref[c, pl.ds(prev, 1), :], carry)
    carry = lax.fori_loop(
        0, prev,
        lambda n, cr: attend(n, None, lambda c, h: sel_ref[c, pl.ds(n, 1), :] + rb_ref[REL_BUCKETS - 1, h] * LOG2E,
                             cr), carry)
    outs = each(lambda c, bi, h: (carry[c][1][:dh] / carry[c][1][dh:dh + 1]).T)
    for bi in range(q_ref.shape[0]):
        o_ref[bi] = jnp.concatenate(outs[bi * nh:(bi + 1) * nh], axis=1).astype(BF16)


def _moba(proj, kmean, rel_bias, bias_tab):
    b, s, _ = proj.shape
    blk, dh, nh = MOBA_BLOCK, MOBA_HEAD_DIM, MOBA_HEADS
    c = nh * dh
    nb = s // blk
    rows = MOBA_BATCH_PER_STEP if b % MOBA_BATCH_PER_STEP == 0 else 1
    nc = rows * nh
    return pl.pallas_call(
        _moba_kernel,
        out_shape=jax.ShapeDtypeStruct((b, s, c), BF16),
        grid=(b // rows, nb),
        in_specs=[pl.BlockSpec(memory_space=pltpu.SMEM),
                  pl.BlockSpec((rows, blk, c), lambda i, j: (i, j, _OFF["mq"] // c)),
                  pl.BlockSpec((rows, blk, c), lambda i, j: (i, j, _OFF["mk"] // c)),
                  pl.BlockSpec((rows, blk, c), lambda i, j: (i, j, _OFF["mv"] // c)),
                  pl.BlockSpec((rows * nb, 1, c), lambda i, j: (i, 0, 0)),
                  pl.BlockSpec((nh, 2, blk, blk), lambda i, j: (0, 0, 0, 0))],
        out_specs=pl.BlockSpec((rows, blk, c), lambda i, j: (i, j, 0)),
        scratch_shapes=[pltpu.VMEM((rows, s, c), BF16), pltpu.VMEM((nc, nb, dh + BF16_SUBLANES, blk), BF16),
                        pltpu.VMEM((nc, nb, blk), F32)],
        compiler_params=_params("parallel", "arbitrary"),
        name="moba",
    )(rel_bias, proj, proj, proj, kmean, bias_tab)


def _pool_branch(cur, halo, pw_ref, ps_ref, start):
    tm = cur.shape[0]
    ext = jnp.concatenate([halo, cur], axis=0)
    pos = start + lax.broadcasted_iota(jnp.int32, (tm, 1), 0)
    outs = []
    for g, w in enumerate(POOL_WINDOWS):
        cols = slice(g * POOL_GROUP_DIM, (g + 1) * POOL_GROUP_DIM)
        a = ext[:, cols]
        shift = 1
        while shift < w:
            a = a + pltpu.roll(a, shift, 0)
            shift *= 2
        cnt = jnp.minimum(pos + 1, w).astype(F32)
        p = a[POOL_HALO:, :] / cnt - cur[:, cols]
        outs.append(_dot(p.astype(BF16), pw_ref[g]))
    return jnp.concatenate(outs, axis=1) * ps_ref[...]


def _mix_kernel(seq, x_ref, ya_ref, yc_ref, u_ref, halo_ref, ga0_ref, ga1_ref, gb0_ref, gb1_ref, gc0_ref, gc1_ref,
                pw_ref, ps_ref, wa_ref, wb_ref, wc_ref, wo_ref, n2_ref, wg_ref, wu_ref, wd_ref, o_ref):
    tm = x_ref.shape[0]
    start = (pl.program_id(0) * tm) % seq
    halo = jnp.where(start > 0, halo_ref[...].astype(F32), 0.0)
    yb = _pool_branch(u_ref[...].astype(F32), halo, pw_ref, ps_ref, start)
    gate = lambda lo_ref, hi_ref: _sigmoid(jnp.concatenate([lo_ref[...], hi_ref[...]], axis=1).astype(F32))
    merged = (gate(ga0_ref, ga1_ref) * _dot(ya_ref[...], wa_ref[...])
              + gate(gb0_ref, gb1_ref) * _dot(yb.astype(BF16), wb_ref[...])
              + gate(gc0_ref, gc1_ref) * _dot(yc_ref[...], wc_ref[...]))
    x1 = x_ref[...] + _dot(merged.astype(BF16), wo_ref[...])
    h = _rms(x1, n2_ref[...]).astype(BF16)
    acc = x1
    for lo in range(0, D_FF, FFN_CHUNK):
        cols = slice(lo, min(lo + FFN_CHUNK, D_FF))
        g = _dot(h, wg_ref[:, cols])
        u = _dot(h, wu_ref[:, cols])
        act = (g * _sigmoid(g) * u).astype(BF16)
        acc = acc + _dot(act, wd_ref[cols, :])
    o_ref[...] = acc


def _mix(x2d, ya, yc, proj2d, seq, layer, pool_w, pool_scale, wa, wb, wc, wo, norm2_w, wg, wu, wd, tm):
    m = x2d.shape[0]
    c = BRANCH_WIDTH
    assert seq % tm == 0 and tm % POOL_HALO == 0 and POOL_HALO == BF16_SUBLANES
    row = lambda w, col: pl.BlockSpec((tm, w), lambda i: (i, col))
    resident = lambda *shape: pl.BlockSpec((None,) + shape, lambda i: (layer,) + (0,) * len(shape),
                                           pipeline_mode=pl.Buffered(1))
    halo_blocks = tm // POOL_HALO
    gates = [row(c, _OFF[g] // c + half) for g in ("ga", "gb", "gc") for half in range(D_MODEL // c)]
    return pl.pallas_call(
        functools.partial(_mix_kernel, seq),
        out_shape=jax.ShapeDtypeStruct((m, D_MODEL), F32),
        grid=(m // tm,),
        in_specs=[row(D_MODEL, 0), row(c, 0), row(c, 0), row(c, _OFF["pu"] // c),
                  pl.BlockSpec((POOL_HALO, c), lambda i: (jnp.maximum(i * halo_blocks - 1, 0), _OFF["pu"] // c)),
                  *gates,
                  resident(POOL_GROUPS, POOL_GROUP_DIM, POOL_GROUP_DIM), resident(1, c),
                  resident(c, D_MODEL), resident(c, D_MODEL), resident(c, D_MODEL), resident(D_MODEL, D_MODEL),
                  resident(1, D_MODEL), resident(D_MODEL, D_FF), resident(D_MODEL, D_FF), resident(D_FF, D_MODEL)],
        out_specs=row(D_MODEL, 0),
        compiler_params=_params("parallel"),
        name="mix",
    )(x2d, ya, yc, proj2d, proj2d, *([proj2d] * len(gates)), pool_w, pool_scale,
      wa, wb, wc, wo, norm2_w, wg, wu, wd)


def _row_tile(m, want):
    t = min(m, want)
    assert m % t == 0
    return t


def kernel(x, norm1_w, w_in, gla_wg2, gla_bg, gla_norm_w, pool_w, pool_scale, moba_qn_w, moba_kn_w,
           rel_bias, w_up_a, w_up_b, w_up_c, w_out, norm2_w, ffn_w_gate, ffn_w_up, ffn_w_down):
    b, s, d = x.shape
    assert d == D_MODEL and s % MOBA_BLOCK == 0 and s % GLA_T == 0
    m = b * s
    bf = lambda a: a.astype(BF16)
    bias_tab = _bias_tables(rel_bias)
    inproj_params = (norm1_w.reshape(DEPTH, 1, d), bf(w_in), moba_qn_w.reshape(DEPTH, 1, MOBA_HEAD_DIM),
                     moba_kn_w.reshape(DEPTH, 1, MOBA_HEAD_DIM))
    mix_weights = (bf(pool_w), pool_scale.reshape(DEPTH, 1, -1), bf(w_up_a), bf(w_up_b), bf(w_up_c), bf(w_out),
                   norm2_w.reshape(DEPTH, 1, d), bf(ffn_w_gate), bf(ffn_w_up), bf(ffn_w_down))
    tm = _row_tile(s, ROW_TILE)
    x2d = x.reshape(m, d)
    for l in range(DEPTH):
        proj2d, kmean = _inproj(x2d, *inproj_params, l, tm)
        proj = proj2d.reshape(b, s, PROJ_COLS)
        ya = _gla(proj, gla_wg2[l], gla_bg[l], gla_norm_w[l])
        yc = _moba(proj, kmean, rel_bias, bias_tab)
        x2d = _mix(x2d, ya.reshape(m, -1), yc.reshape(m, -1), proj2d, s, l, *mix_weights, tm)
    return x2d.reshape(b, s, d)
```
